```python
import jax
import jax.numpy as jnp
from jax import lax
import numpy as np

D_MODEL = 1024
BATCH = 8
SEQ = 4096
DEPTH = 4

HEAD_DIM = 64
D_MIX = D_MODEL
GROUP_WIDTH = D_MIX // 4
SWA_HEADS = GROUP_WIDTH // HEAD_DIM
SWA_KV_HEADS = 2
SWA_WINDOW = 128
ROPE_THETA = 10000.0
GLA_HEADS = 4
GLA_DK = 32
GLA_DV = GROUP_WIDTH // GLA_HEADS
GLA_LOW_RANK = 16
GLA_TAU = 16.0
GLA_CHUNK = 64
SB_HEADS = GROUP_WIDTH // HEAD_DIM
SB_BLOCK = 128
RWKV_HEADS = GROUP_WIDTH // HEAD_DIM
RWKV_DECAY_LORA = 64
RWKV_A_LORA = 64
RWKV_GATE_LORA = 128
RWKV_DECAY_SCALE = 0.6065306597126334
RWKV_GN_EPS = 64e-5
D_FF = 2816
N_EXPERTS = 8
TOP_K = 2
D_FF_EXPERT = 1408
NORM_EPS = 1e-6
N_DENSE_LAYERS = (DEPTH + 1) // 2
N_MOE_LAYERS = DEPTH // 2

A_Q = SWA_HEADS * HEAD_DIM
A_KV = SWA_KV_HEADS * HEAD_DIM
A_COLS = A_Q + 2 * A_KV
B_QK = GLA_HEADS * GLA_DK
B_V = GLA_HEADS * GLA_DV
B_COLS = 2 * B_QK + 2 * B_V + GLA_LOW_RANK
C_COLS = 3 * GROUP_WIDTH
D_COLS = 3 * GROUP_WIDTH + RWKV_DECAY_LORA + RWKV_A_LORA + RWKV_GATE_LORA
IN_COLS = A_COLS + B_COLS + C_COLS + D_COLS

kernel_name = 'hymba_style_four_mixer_hybrid'

F32 = jnp.float32


def rmsnorm(x, g, eps=NORM_EPS):
    xf = x.astype(F32)
    y = xf * lax.rsqrt(jnp.mean(xf * xf, axis=-1, keepdims=True) + eps)
    return (y * g.astype(F32)).astype(x.dtype)


def rope_tables(positions):
    inv_freq = ROPE_THETA ** (-jnp.arange(0, HEAD_DIM, 2, dtype=F32) / HEAD_DIM)
    ang = positions.astype(F32)[..., None] * inv_freq
    return jnp.cos(ang), jnp.sin(ang)


def apply_rope(x, cos, sin):
    xf = x.astype(F32)
    x1, x2 = jnp.split(xf, 2, axis=-1)
    c, s = cos[:, :, None], sin[:, :, None]
    return jnp.concatenate([x1 * c - x2 * s, x2 * c + x1 * s], axis=-1).astype(x.dtype)


def swa_sink_attention(q, k, v, sinks):
    dtype = q.dtype
    b_, s_ = q.shape[:2]
    nb = s_ // SWA_WINDOW
    grp = SWA_HEADS // SWA_KV_HEADS
    qb = q.astype(F32).reshape(b_, nb, SWA_WINDOW, SWA_KV_HEADS, grp, HEAD_DIM)

    def band(t):
        tb = t.astype(F32).reshape(b_, nb, SWA_WINDOW, SWA_KV_HEADS, HEAD_DIM)
        prev = jnp.concatenate([jnp.zeros_like(tb[:, :1]), tb[:, :-1]], axis=1)
        return jnp.concatenate([prev, tb], axis=2)

    kw, vw = band(k), band(v)
    scores = jnp.einsum('bnqhgd,bnjhd->bnhgqj', qb, kw) * HEAD_DIM ** -0.5
    i = jnp.arange(SWA_WINDOW)[:, None]
    j = jnp.arange(2 * SWA_WINDOW)[None, :]
    in_band = (j > i) & (j <= i + SWA_WINDOW)
    blk = jnp.arange(nb)[:, None, None]
    valid = in_band[None] & ((blk > 0) | (j[None] >= SWA_WINDOW))
    scores = jnp.where(valid[None, :, None, None], scores, -jnp.inf)
    sink = sinks.astype(F32).reshape(SWA_KV_HEADS, grp)[None, None, :, :, None, None]
    m = jnp.maximum(scores.max(axis=-1, keepdims=True), sink)
    p = jnp.exp(scores - m)
    w = p / (p.sum(axis=-1, keepdims=True) + jnp.exp(sink - m))
    o = jnp.einsum('bnhgqj,bnjhd->bnqhgd', w, vw)
    return o.reshape(b_, s_, SWA_HEADS, HEAD_DIM).astype(dtype)


def gla_chunked(q, k, v, log_a):
    dtype = v.dtype
    b_, s_, h, dk = q.shape
    dv = v.shape[-1]
    nc = s_ // GLA_CHUNK

    def chunks(t):
        return t.astype(F32).reshape(b_, nc, GLA_CHUNK, h, t.shape[-1]).transpose(0, 3, 1, 2, 4)

    q, k, v, la = chunks(q) * dk ** -0.5, chunks(k), chunks(v), chunks(log_a)
    b = jnp.cumsum(la, axis=3)
    b_ref = b[:, :, :, GLA_CHUNK // 2 - 1:GLA_CHUNK // 2]
    b_last = b[:, :, :, -1:]
    causal = jnp.tril(jnp.ones((GLA_CHUNK, GLA_CHUNK), dtype=bool))
    att = jnp.einsum('bhnte,bhnse->bhnts', q * jnp.exp(b - b_ref), k * jnp.exp(b_ref - b))
    att = jnp.where(causal, att, 0.0)
    o = jnp.einsum('bhnts,bhnsf->bhntf', att, v)
    kv = jnp.einsum('bhnse,bhnsf->bhnef', k * jnp.exp(b_last - b), v)
    decay = jnp.exp(b_last[:, :, :, 0])

    def step(state, inp):
        dec, kv_n = inp
        return dec[..., None] * state + kv_n, state

    _, s_prev = lax.scan(step, jnp.zeros((b_, h, dk, dv), F32),
                         (jnp.moveaxis(decay, 2, 0), jnp.moveaxis(kv, 2, 0)))
    o = o + jnp.einsum('bhnte,nbhef->bhntf', q * jnp.exp(b), s_prev)
    return o.transpose(0, 2, 3, 1, 4).reshape(b_, s_, h, dv).astype(dtype)


def stick_breaking_attention(q, k, v):
    dtype = v.dtype
    b_, s_, h, dh = q.shape
    q, k, v = (t.astype(F32).transpose(0, 2, 1, 3) for t in (q, k, v))
    outs = []
    for blk in range(s_ // SB_BLOCK):
        start, stop = blk * SB_BLOCK, (blk + 1) * SB_BLOCK
        z = jnp.einsum('bhtd,bhsd->bhts', q[:, :, start:stop], k[:, :, :stop]) * dh ** -0.5
        strict = jnp.arange(stop)[None, :] < jnp.arange(start, stop)[:, None]
        log_1mb = jnp.where(strict, jax.nn.log_sigmoid(-z), 0.0)
        between = lax.cumsum(log_1mb, axis=3, reverse=True) - log_1mb
        attn = jnp.where(strict, jnp.exp(jax.nn.log_sigmoid(z) + between), 0.0)
        outs.append(jnp.einsum('bhts,bhsd->bhtd', attn, v[:, :, :stop]))
    o = jnp.concatenate(outs, axis=2)
    return o.transpose(0, 2, 1, 3).astype(dtype)


def rwkv7_time_mix(pd, mu, w0, w2, a0, a2, g2, k_k, k_a, r_k, ln_w, ln_b):
    dtype = pd.dtype
    b_, s_, _ = pd.shape
    pd = pd.astype(F32)
    shifted = jnp.concatenate([jnp.zeros_like(pd[:, :1]), pd[:, :-1]], axis=1)
    pd = pd + (shifted - pd) * mu
    gw = GROUP_WIDTH
    r, k, v, wl, al, gl = jnp.split(
        pd, [gw, 2 * gw, 3 * gw, 3 * gw + RWKV_DECAY_LORA, 3 * gw + RWKV_DECAY_LORA + RWKV_A_LORA], axis=-1)
    log_w = -RWKV_DECAY_SCALE * jax.nn.sigmoid(w0 + jnp.tanh(wl) @ w2)
    a = jax.nn.sigmoid(a0 + al @ a2)
    g = jax.nn.sigmoid(gl) @ g2

    def heads(t):
        return t.reshape(b_, s_, RWKV_HEADS, HEAD_DIM)

    kk = heads(k * k_k)
    kk = kk * lax.rsqrt(jnp.sum(kk * kk, axis=-1, keepdims=True) + 1e-12)
    k = k * (1.0 + (a - 1.0) * k_a)
    r, k, v, a, decay = heads(r), heads(k), heads(v), heads(a), heads(jnp.exp(log_w))

    def step(state, inp):
        r_t, w_t, k_t, v_t, kk_t, b_t = inp
        sa = jnp.einsum('bhvk,bhk->bhv', state, kk_t)
        state = (state * w_t[:, :, None, :] - sa[..., None] * b_t[:, :, None, :]
                 + v_t[..., None] * k_t[:, :, None, :])
        return state, jnp.einsum('bhvk,bhk->bhv', state, r_t)

    seq_first = lambda t: jnp.moveaxis(t, 1, 0)
    _, y = lax.scan(step, jnp.zeros((b_, RWKV_HEADS, HEAD_DIM, HEAD_DIM), F32),
                    (seq_first(r), seq_first(decay), seq_first(k), seq_first(v), seq_first(kk), seq_first(kk * a)))
    y = jnp.moveaxis(y, 0, 1)
    mean = jnp.mean(y, axis=-1, keepdims=True)
    var = jnp.mean(jnp.square(y - mean), axis=-1, keepdims=True)
    y = (y - mean) * lax.rsqrt(var + RWKV_GN_EPS) * ln_w + ln_b
    y = y + jnp.sum(r * k * r_k, axis=-1, keepdims=True) * v
    return (y.reshape(b_, s_, GROUP_WIDTH) * g).astype(dtype)


def swiglu(h, w1, w3, w2):
    return (jax.nn.silu(h @ w1) * (h @ w3)) @ w2


def moe_swiglu(h, router, w1, w3, w2):
    logits = (h @ router).astype(F32)
    top_vals, top_idx = lax.top_k(logits, TOP_K)
    gates = jax.nn.softmax(top_vals, axis=-1)
    combine = jnp.sum(jax.nn.one_hot(top_idx, N_EXPERTS, dtype=F32) * gates[..., None], axis=-2)
    out = jnp.zeros_like(h)
    for e in range(N_EXPERTS):
        out = out + combine[..., e:e + 1].astype(h.dtype) * swiglu(h, w1[e], w3[e], w2[e])
    return out


def setup_inputs(seed: int = 0) -> dict:
    key = jax.random.key(seed)
    ks = iter(jax.random.split(key, 48))
    L = DEPTH

    def nrm(shape, scale):
        return jax.random.normal(next(ks), shape, F32) * scale

    def gain(shape):
        return 1.0 + nrm(shape, 0.02)

    x = nrm((BATCH, SEQ, D_MODEL), 1.0)
    offsets = jax.random.randint(next(ks), (BATCH, 1), 0, 1024, dtype=jnp.int32)
    positions = offsets + jnp.arange(SEQ, dtype=jnp.int32)[None, :]
    return {
        'x': x,
        'positions': positions,
        'attn_norm': gain((L, D_MODEL)),
        'w_in': nrm((L, D_MODEL, IN_COLS), D_MODEL ** -0.5),
        'swa_sinks': nrm((L, SWA_HEADS), 0.5),
        'swa_out_norm': gain((L, SWA_HEADS, HEAD_DIM)),
        'gla_w_a2': nrm((L, GLA_LOW_RANK, B_QK), GLA_LOW_RANK ** -0.5),
        'gla_b_a': 1.0 + nrm((L, B_QK), 0.5),
        'gla_out_norm': gain((L, GLA_HEADS, GLA_DV)),
        'sb_out_norm': gain((L, SB_HEADS, HEAD_DIM)),
        'rwkv_mu': jax.random.uniform(next(ks), (L, D_COLS), F32),
        'rwkv_w0': nrm((L, GROUP_WIDTH), 0.5),
        'rwkv_w2': nrm((L, RWKV_DECAY_LORA, GROUP_WIDTH), 0.5 * RWKV_DECAY_LORA ** -0.5),
        'rwkv_a0': nrm((L, GROUP_WIDTH), 0.5),
        'rwkv_a2': nrm((L, RWKV_A_LORA, GROUP_WIDTH), 0.5 * RWKV_A_LORA ** -0.5),
        'rwkv_g2': nrm((L, RWKV_GATE_LORA, GROUP_WIDTH), RWKV_GATE_LORA ** -0.5),
        'rwkv_k_k': 0.85 + nrm((L, GROUP_WIDTH), 0.05),
        'rwkv_k_a': 1.0 + nrm((L, GROUP_WIDTH), 0.05),
        'rwkv_r_k': nrm((L, RWKV_HEADS, HEAD_DIM), 0.1),
        'rwkv_ln_w': gain((L, RWKV_HEADS, HEAD_DIM)),
        'rwkv_ln_b': nrm((L, RWKV_HEADS, HEAD_DIM), 0.02),
        'w_out': nrm((L, D_MIX, D_MODEL), D_MIX ** -0.5),
        'ffn_norm': gain((L, D_MODEL)),
        'ffn_w1': nrm((N_DENSE_LAYERS, D_MODEL, D_FF), D_MODEL ** -0.5),
        'ffn_w3': nrm((N_DENSE_LAYERS, D_MODEL, D_FF), D_MODEL ** -0.5),
        'ffn_w2': nrm((N_DENSE_LAYERS, D_FF, D_MODEL), D_FF ** -0.5),
        'router': nrm((N_MOE_LAYERS, D_MODEL, N_EXPERTS), D_MODEL ** -0.5),
        'moe_w1': nrm((N_MOE_LAYERS, N_EXPERTS, D_MODEL, D_FF_EXPERT), D_MODEL ** -0.5),
        'moe_w3': nrm((N_MOE_LAYERS, N_EXPERTS, D_MODEL, D_FF_EXPERT), D_MODEL ** -0.5),
        'moe_w2': nrm((N_MOE_LAYERS, N_EXPERTS, D_FF_EXPERT, D_MODEL), D_FF_EXPERT ** -0.5),
        'final_norm': gain((D_MODEL,)),
    }


def reference(x, positions, attn_norm, w_in, swa_sinks, swa_out_norm, gla_w_a2, gla_b_a, gla_out_norm,
              sb_out_norm, rwkv_mu, rwkv_w0, rwkv_w2, rwkv_a0, rwkv_a2, rwkv_g2, rwkv_k_k, rwkv_k_a,
              rwkv_r_k, rwkv_ln_w, rwkv_ln_b, w_out, ffn_norm, ffn_w1, ffn_w3, ffn_w2, router,
              moe_w1, moe_w3, moe_w2, final_norm):
    b_, s_, _ = x.shape
    cos, sin = rope_tables(positions)
    for l in range(DEPTH):
        h = rmsnorm(x, attn_norm[l])
        proj = h @ w_in[l]
        pa, pb, pc, pd = jnp.split(proj, [A_COLS, A_COLS + B_COLS, A_COLS + B_COLS + C_COLS], axis=-1)

        qa, ka, va = jnp.split(pa, [A_Q, A_Q + A_KV], axis=-1)
        qa = apply_rope(qa.reshape(b_, s_, SWA_HEADS, HEAD_DIM), cos, sin)
        ka = apply_rope(ka.reshape(b_, s_, SWA_KV_HEADS, HEAD_DIM), cos, sin)
        oa = swa_sink_attention(qa, ka, va.reshape(b_, s_, SWA_KV_HEADS, HEAD_DIM), swa_sinks[l])
        oa = rmsnorm(oa, swa_out_norm[l]).reshape(b_, s_, GROUP_WIDTH)

        qb, kb, vb, gb, lrb = jnp.split(pb, [B_QK, 2 * B_QK, 2 * B_QK + B_V, 2 * B_QK + 2 * B_V], axis=-1)
        log_a = jax.nn.log_sigmoid((lrb @ gla_w_a2[l] + gla_b_a[l]).astype(F32)) / GLA_TAU
        ob = gla_chunked(qb.reshape(b_, s_, GLA_HEADS, GLA_DK), kb.reshape(b_, s_, GLA_HEADS, GLA_DK),
                         vb.reshape(b_, s_, GLA_HEADS, GLA_DV), log_a.reshape(b_, s_, GLA_HEADS, GLA_DK))
        ob = rmsnorm(ob, gla_out_norm[l]).reshape(b_, s_, GROUP_WIDTH) * jax.nn.silu(gb)

        qc, kc, vc = jnp.split(pc, 3, axis=-1)
        oc = stick_breaking_attention(qc.reshape(b_, s_, SB_HEADS, HEAD_DIM), kc.reshape(b_, s_, SB_HEADS, HEAD_DIM),
                                      vc.reshape(b_, s_, SB_HEADS, HEAD_DIM))
        oc = rmsnorm(oc, sb_out_norm[l]).reshape(b_, s_, GROUP_WIDTH)

        od = rwkv7_time_mix(pd, rwkv_mu[l], rwkv_w0[l], rwkv_w2[l], rwkv_a0[l], rwkv_a2[l], rwkv_g2[l],
                            rwkv_k_k[l], rwkv_k_a[l], rwkv_r_k[l], rwkv_ln_w[l], rwkv_ln_b[l])

        mix = jnp.concatenate([oa, ob, oc, od], axis=-1).astype(x.dtype)
        x = x + mix @ w_out[l]

        h = rmsnorm(x, ffn_norm[l])
        if l % 2 == 0:
            x = x + swiglu(h, ffn_w1[l // 2], ffn_w3[l // 2], ffn_w2[l // 2])
        else:
            x = x + moe_swiglu(h, router[l // 2], moe_w1[l // 2], moe_w3[l // 2], moe_w2[l // 2])
    return rmsnorm(x, final_norm)
```

```python
import functools

import jax
import jax.numpy as jnp
from jax import lax
from jax.experimental import pallas as pl
from jax.experimental.pallas import tpu as pltpu

F32 = jnp.float32
BF16 = jnp.bfloat16

D_MODEL = 1024
HEAD_DIM = 64
GROUP_WIDTH = 256
SWA_WINDOW = 128
ROPE_THETA = 10000.0
GLA_HEADS = 4
GLA_DK = 32
GLA_LOW_RANK = 16
GLA_TAU = 16.0
GLA_CHUNK = 64
SB_BLOCK = 128
RWKV_CHUNK = 64
RWKV_DECAY_SCALE = 0.6065306597126334
RWKV_GN_EPS = 64e-5
D_FF = 2816
N_EXPERTS = 8
D_FF_EXPERT = 1408
NORM_EPS = 1e-6

A_COLS = 512
B_COLS = 784
C_COLS = 768
D_COLS = 1024
B_COLS_PAD = 896

LANES = 128
VMEM_LIMIT_BYTES = 56 * 1024 * 1024


def _params(*sem):
    return pltpu.CompilerParams(dimension_semantics=sem, vmem_limit_bytes=VMEM_LIMIT_BYTES)


def _dot(a, b):
    return jnp.dot(a, b, preferred_element_type=F32)


def _dot_nt(a, b):
    return lax.dot_general(a, b, (((1,), (1,)), ((), ())), preferred_element_type=F32)


def _dot_tn(a, b):
    return lax.dot_general(a, b, (((0,), (0,)), ((), ())), preferred_element_type=F32)


def _dot_hi(a, b):
    return jnp.dot(a, b, preferred_element_type=F32, precision=lax.Precision.HIGHEST)


def _dot_nt_hi(a, b):
    return lax.dot_general(a, b, (((1,), (1,)), ((), ())), preferred_element_type=F32,
                           precision=lax.Precision.HIGHEST)


def _split_dot(a, b_exact):
    hi = a.astype(BF16)
    lo = (a - hi.astype(F32)).astype(BF16)
    return _dot(hi, b_exact) + _dot(lo, b_exact)


def _split_dot_tri(tri, a):
    hi = a.astype(BF16)
    lo = (a - hi.astype(F32)).astype(BF16)
    return _dot(tri, hi) + _dot(tri, lo)


def _iota(shape, dim):
    return lax.broadcasted_iota(jnp.int32, shape, dim)


def _idiv(x, d):
    return jnp.right_shift(x, d.bit_length() - 1)


def _imod(x, d):
    return jnp.bitwise_and(x, d - 1)


def _group_ones(width, group):
    r = _idiv(_iota((width, width), 0), group)
    c = _idiv(_iota((width, width), 1), group)
    return jnp.where(r == c, 1.0, 0.0).astype(BF16)


def _softplus(z):
    return jnp.maximum(z, 0.0) + jnp.log1p(jnp.exp(-jnp.abs(z)))


def _sigmoid(z):
    return 1.0 / (1.0 + jnp.exp(-z))


def _rms_scale(x):
    return x * lax.rsqrt(jnp.mean(x * x, axis=-1, keepdims=True) + NORM_EPS)


def _in_proj_kernel(x_ref, g_ref, wa_ref, wb_ref, wc_ref, wd_ref, pa_ref, pb_ref, pc_ref, pd_ref):
    h = (_rms_scale(x_ref[...]) * g_ref[...]).astype(BF16)
    pa_ref[...] = _dot(h, wa_ref[...])
    pb_ref[...] = _dot(h, wb_ref[...])
    pc_ref[...] = _dot(h, wc_ref[...]).astype(BF16)
    pd_ref[...] = _dot(h, wd_ref[...])


def _in_proj(x2, g, wa, wb, wc, wd, tm=512):
    n = x2.shape[0]
    full = lambda w: pl.BlockSpec(w.shape, lambda i: (0, 0))
    row = lambda c: pl.BlockSpec((tm, c), lambda i: (i, 0))
    return pl.pallas_call(
        _in_proj_kernel,
        grid=(n // tm,),
        in_specs=[row(D_MODEL), full(g), full(wa), full(wb), full(wc), full(wd)],
        out_specs=[row(A_COLS), row(B_COLS_PAD), row(C_COLS), row(D_COLS)],
        out_shape=[jax.ShapeDtypeStruct((n, A_COLS), F32), jax.ShapeDtypeStruct((n, B_COLS_PAD), F32),
                   jax.ShapeDtypeStruct((n, C_COLS), BF16), jax.ShapeDtypeStruct((n, D_COLS), F32)],
        compiler_params=_params("parallel"),
        name="in_proj",
    )(x2, g, wa, wb, wc, wd)


def _swa_kernel(sink_ref, cur_ref, prev_ref, cc_ref, sc_ref, cp_ref, sp_ref, gain_ref, o_ref):
    w = SWA_WINDOW
    blk = pl.program_id(1)
    lane = _iota((1, LANES), 1)
    first_half = _imod(lane, HEAD_DIM) < (HEAD_DIM // 2)
    low_head = lane < HEAD_DIM

    def rope(x, c, s):
        rot = jnp.where(first_half, pltpu.roll(x, LANES - HEAD_DIM // 2, 1), pltpu.roll(x, HEAD_DIM // 2, 1))
        return x * c + rot * s

    cc, sc = cc_ref[...], sc_ref[...]
    scale = HEAD_DIM ** -0.5
    q01 = rope(cur_ref[:, 0:128], cc, sc) * scale
    q23 = rope(cur_ref[:, 128:256], cc, sc) * scale
    k_cur = rope(cur_ref[:, 256:384], cc, sc)
    k_prev = rope(prev_ref[:, 256:384], cp_ref[...], sp_ref[...])
    kw = jnp.concatenate([k_prev, k_cur], axis=0).astype(BF16)
    vw = jnp.concatenate([prev_ref[:, 384:512], cur_ref[:, 384:512]], axis=0).astype(BF16)
    qs = jnp.concatenate([
        jnp.where(low_head, q01, 0.0),
        jnp.where(low_head, pltpu.roll(q01, HEAD_DIM, 1), 0.0),
        jnp.where(low_head, 0.0, pltpu.roll(q23, HEAD_DIM, 1)),
        jnp.where(low_head, 0.0, q23)], axis=0).astype(BF16)
    s_all = _dot_nt(qs, kw)

    qi = _iota((w, 2 * w), 0)
    kj = _iota((w, 2 * w), 1)
    valid = (kj > qi) & (kj <= qi + w) & ((blk > 0) | (kj >= w))
    heads = []
    for h in range(4):
        s = s_all[h * w:(h + 1) * w]
        sink = sink_ref[h]
        m = jnp.maximum(jnp.max(jnp.where(valid, s, -1e30), axis=-1, keepdims=True), sink)
        p = jnp.where(valid, jnp.exp(s - m), 0.0)
        den = jnp.sum(p, axis=-1, keepdims=True) + jnp.exp(sink - m)
        o = _dot(p.astype(BF16), vw) / den
        own = low_head if h < 2 else jnp.logical_not(low_head)
        ms = jnp.sum(jnp.where(own, o * o, 0.0), axis=-1, keepdims=True) * (1.0 / HEAD_DIM)
        heads.append(o * lax.rsqrt(ms + NORM_EPS))
    out01 = jnp.where(low_head, heads[0], pltpu.roll(heads[1], HEAD_DIM, 1))
    out23 = jnp.where(low_head, pltpu.roll(heads[2], HEAD_DIM, 1), heads[3])
    o_ref[...] = (jnp.concatenate([out01, out23], axis=1) * gain_ref[...]).astype(o_ref.dtype)


def _swa(pa, cos_t, sin_t, sinks, gain):
    b, s, _ = pa.shape
    w = SWA_WINDOW
    cur = lambda c: pl.BlockSpec((None, w, c), lambda bi, i: (bi, i, 0))
    prev = lambda c: pl.BlockSpec((None, w, c), lambda bi, i: (bi, jnp.maximum(i - 1, 0), 0))
    return pl.pallas_call(
        _swa_kernel,
        grid=(b, s // w),
        in_specs=[pl.BlockSpec(memory_space=pltpu.SMEM), cur(A_COLS), prev(A_COLS),
                  cur(LANES), cur(LANES), prev(LANES), prev(LANES),
                  pl.BlockSpec((1, GROUP_WIDTH), lambda bi, i: (0, 0))],
        out_specs=cur(GROUP_WIDTH),
        out_shape=jax.ShapeDtypeStruct((b, s, GROUP_WIDTH), BF16),
        compiler_params=_params("parallel", "parallel"),
        name="swa",
    )(sinks, pa, pa, cos_t, sin_t, cos_t, sin_t, gain)


def _gla_kernel(pb_ref, wa2_ref, ba_ref, gain_ref, o_ref, st_ref, ob_ref, la_ref):
    c_len = GLA_CHUNK
    t_len = pb_ref.shape[0]
    nh = GLA_HEADS

    @pl.when(pl.program_id(1) == 0)
    def _():
        st_ref[...] = jnp.zeros_like(st_ref)

    lr = pb_ref[:, 768:896].astype(BF16)
    la_ref[...] = -_softplus(-(_dot(lr, wa2_ref[...]) + ba_ref[...])) * (1.0 / GLA_TAU)

    tri = jnp.where(_iota((c_len, c_len), 0) >= _iota((c_len, c_len), 1), 1.0, 0.0).astype(BF16)
    klane = _idiv(_iota((1, LANES), 1), GLA_DK)
    vlane = _idiv(_iota((1, GROUP_WIDTH), 1), HEAD_DIM)
    r_i = _iota((nh * c_len, nh * c_len), 0)
    c_i = _iota((nh * c_len, nh * c_len), 1)
    causal = (_idiv(r_i, c_len) == _idiv(c_i, c_len)) & (_imod(r_i, c_len) >= _imod(c_i, c_len))
    st_mask = _idiv(_iota((GROUP_WIDTH, LANES), 0), HEAD_DIM) == _idiv(_iota((GROUP_WIDTH, LANES), 1), GLA_DK)

    def stack_k(x):
        return jnp.concatenate([jnp.where(klane == h, x, 0.0) for h in range(nh)], axis=0).astype(BF16)

    def chunk(ci, carry):
        rows = pl.ds(pl.multiple_of(ci * c_len, c_len), c_len)
        b = _split_dot_tri(tri, la_ref[rows, :])
        b_ref = b[c_len // 2 - 1:c_len // 2]
        b_last = b[c_len - 1:c_len]
        q = pb_ref[rows, 0:128] * (GLA_DK ** -0.5)
        k = pb_ref[rows, 128:256]
        v = pb_ref[rows, 256:512]
        att = _dot_nt(stack_k(q * jnp.exp(b - b_ref)), stack_k(k * jnp.exp(b_ref - b)))
        att = jnp.where(causal, att, 0.0).astype(BF16)
        v_st = jnp.concatenate([jnp.where(vlane == h, v, 0.0) for h in range(nh)], axis=0).astype(BF16)
        st = st_ref[...]
        o_st = _dot(att, v_st) + _dot_nt(stack_k(q * jnp.exp(b)), st.astype(BF16))
        o = o_st[0:c_len]
        for h in range(1, nh):
            o = o + o_st[h * c_len:(h + 1) * c_len]
        ob_ref[rows, :] = o
        kv = _dot_tn(v.astype(BF16), (k * jnp.exp(b_last - b)).astype(BF16))
        st_ref[...] = st * jnp.exp(b_last) + jnp.where(st_mask, kv, 0.0)
        return carry

    lax.fori_loop(0, t_len // c_len, chunk, 0)

    ob = ob_ref[...]
    ms = _split_dot(ob * ob, _group_ones(GROUP_WIDTH, HEAD_DIM)) * (1.0 / HEAD_DIM)
    g = pb_ref[:, 512:768]
    y = ob * lax.rsqrt(ms + NORM_EPS) * gain_ref[...] * (g * _sigmoid(g))
    o_ref[...] = y.astype(o_ref.dtype)


def _gla(pb, wa2, ba, gain, tile=512):
    b, s, _ = pb.shape
    full = lambda a: pl.BlockSpec(a.shape, lambda bi, i: (0, 0))
    return pl.pallas_call(
        _gla_kernel,
        grid=(b, s // tile),
        in_specs=[pl.BlockSpec((None, tile, B_COLS_PAD), lambda bi, i: (bi, i, 0)), full(wa2), full(ba), full(gain)],
        out_specs=pl.BlockSpec((None, tile, GROUP_WIDTH), lambda bi, i: (bi, i, 0)),
        out_shape=jax.ShapeDtypeStruct((b, s, GROUP_WIDTH), BF16),
        scratch_shapes=[pltpu.VMEM((GROUP_WIDTH, LANES), F32), pltpu.VMEM((tile, GROUP_WIDTH), F32),
                        pltpu.VMEM((tile, LANES), F32)],
        compiler_params=_params("parallel", "arbitrary"),
        name="gla",
    )(pb, wa2, ba, gain)


def _sb_kernel(pc_ref, gain_ref, o_ref, kcat_ref, vcat_ref):
    t = SB_BLOCK
    nh = 4
    qb = pl.program_id(1)
    nblk = pc_ref.shape[0] // t
    hlane = _idiv(_iota((1, GROUP_WIDTH), 1), HEAD_DIM)

    @pl.when(qb == 0)
    def _():
        def fill(kb, carry):
            rows = pl.ds(pl.multiple_of(kb * t, t), t)
            k = pc_ref[rows, 256:512]
            v = pc_ref[rows, 512:768]
            for h in range(nh):
                kcat_ref[kb, h * t:(h + 1) * t, :] = jnp.where(hlane == h, k, jnp.zeros_like(k))
                vcat_ref[kb, h * t:(h + 1) * t, :] = jnp.where(hlane == h, v, jnp.zeros_like(v))
            return carry
        lax.fori_loop(0, nblk, fill, 0)

    q = pc_ref[pl.ds(pl.multiple_of(qb * t, t), t), 0:256] * (HEAD_DIM ** -0.5)
    uj = jnp.where((_iota((t, 2 * t), 0) > _iota((t, 2 * t), 1)) | (_iota((t, 2 * t), 1) >= t), 1.0, 0.0).astype(BF16)
    strict = _imod(_iota((t, nh * t), 1), t) < _iota((t, nh * t), 0)

    def block(kb, c, acc, diag):
        z = _dot_nt(q, kcat_ref[kb])
        sp = _softplus(z)
        if diag:
            sp = jnp.where(strict, sp, 0.0)
        args, c_new = [], []
        for h in range(nh):
            sl = slice(h * t, (h + 1) * t)
            cj = _split_dot(sp[:, sl], uj)
            args.append(z[:, sl] - sp[:, sl] - cj[:, :t] - c[h])
            c_new.append(c[h] + cj[:, t:])
        attn = jnp.exp(jnp.concatenate(args, axis=1))
        if diag:
            attn = jnp.where(strict, attn, 0.0)
        acc = acc + _dot(attn.astype(BF16), vcat_ref[kb])
        return tuple(c_new), acc

    zero = jnp.zeros((t, t), F32)
    c, acc = block(qb, (zero,) * nh, jnp.zeros((t, GROUP_WIDTH), F32), True)

    def body(i, carry):
        return block(qb - 1 - i, carry[0], carry[1], False)

    c, acc = lax.fori_loop(0, qb, body, (c, acc))
    ms = _split_dot(acc * acc, _group_ones(GROUP_WIDTH, HEAD_DIM)) * (1.0 / HEAD_DIM)
    o_ref[...] = (acc * lax.rsqrt(ms + NORM_EPS) * gain_ref[...]).astype(o_ref.dtype)


def _sb(pc, gain):
    b, s, _ = pc.shape
    t = SB_BLOCK
    return pl.pallas_call(
        _sb_kernel,
        grid=(b, s // t),
        in_specs=[pl.BlockSpec((None, s, C_COLS), lambda bi, i: (bi, 0, 0)),
                  pl.BlockSpec((1, GROUP_WIDTH), lambda bi, i: (0, 0))],
        out_specs=pl.BlockSpec((None, t, GROUP_WIDTH), lambda bi, i: (bi, i, 0)),
        out_shape=jax.ShapeDtypeStruct((b, s, GROUP_WIDTH), BF16),
        scratch_shapes=[pltpu.VMEM((s // t, 4 * t, GROUP_WIDTH), BF16), pltpu.VMEM((s // t, 4 * t, GROUP_WIDTH), BF16)],
        compiler_params=_params("parallel", "arbitrary"),
        name="stick_breaking",
    )(pc, gain)


def _rwkv_kernel(pd_ref, mu_ref, w0_ref, w2_ref, a0_ref, a2_ref, g2_ref, kk_ref, ka_ref, rk_ref, lnw_ref, lnb_ref,
                 o_ref, st_ref, carry_ref, r_s, k_s, v_s, kk_s, b_s, lw_s, y_s):
    c_len = RWKV_CHUNK
    t_len = pd_ref.shape[0]
    gw = GROUP_WIDTH

    @pl.when(pl.program_id(1) == 0)
    def _():
        st_ref[...] = jnp.zeros_like(st_ref)
        carry_ref[...] = jnp.zeros_like(carry_ref)

    cur = pd_ref[...]
    shifted = jnp.where(_iota((t_len, 1), 0) == 0, carry_ref[7:8, :], pltpu.roll(cur, 1, 0))
    carry_ref[...] = cur[t_len - 8:t_len, :]
    xm = cur + (shifted - cur) * mu_ref[...]
    r = xm[:, 0:gw]
    k = xm[:, gw:2 * gw]
    v = xm[:, 2 * gw:3 * gw]
    wa = xm[:, 3 * gw:3 * gw + 128]
    gl = xm[:, 3 * gw + 128:4 * gw]
    lora_lo = _iota((1, LANES), 1) < 64
    wl = jnp.where(lora_lo, jnp.tanh(wa), 0.0).astype(BF16)
    lw_s[...] = -RWKV_DECAY_SCALE * _sigmoid(w0_ref[...] + _dot(wl, w2_ref[...]))
    a = _sigmoid(a0_ref[...] + _dot(wa.astype(BF16), a2_ref[...]))
    g = _dot(_sigmoid(gl).astype(BF16), g2_ref[...])
    ones = _group_ones(gw, HEAD_DIM)
    kk = k * kk_ref[...]
    kk = kk * lax.rsqrt(_split_dot(kk * kk, ones) + 1e-12)
    k2 = k * (1.0 + (a - 1.0) * ka_ref[...])
    r_s[...] = r
    k_s[...] = k2
    v_s[...] = v
    kk_s[...] = kk
    b_s[...] = kk * a
    bonus = _split_dot(r * k2 * rk_ref[...], ones) * v

    tri = jnp.where(_iota((c_len, c_len), 0) >= _iota((c_len, c_len), 1), 1.0, 0.0).astype(BF16)
    head0 = _iota((1, LANES), 1) < HEAD_DIM
    r_i = _iota((2 * c_len, 2 * c_len), 0)
    c_i = _iota((2 * c_len, 2 * c_len), 1)
    same = _idiv(r_i, c_len) == _idiv(c_i, c_len)
    strict = same & (_imod(r_i, c_len) > _imod(c_i, c_len))
    incl = same & (_imod(r_i, c_len) >= _imod(c_i, c_len))
    eye = jnp.where(r_i == c_i, 1.0, 0.0)

    def stack(x):
        return jnp.concatenate([jnp.where(head0, x, 0.0), jnp.where(head0, 0.0, x)], axis=0)

    def chunk(ci, carry):
        rows = pl.ds(pl.multiple_of(ci * c_len, c_len), c_len)
        for p in range(2):
            ln = slice(p * LANES, (p + 1) * LANES)
            lw = lw_s[rows, ln]
            cum = _split_dot_tri(tri, lw)
            ref = cum[c_len // 2 - 1:c_len // 2]
            last = cum[c_len - 1:c_len]
            e_neg = jnp.exp(ref - cum)
            rd = r_s[rows, ln] * jnp.exp(cum - ref)
            kkd = kk_s[rows, ln] * jnp.exp(cum - lw - ref)
            kd = k_s[rows, ln] * e_neg
            bd = b_s[rows, ln] * e_neg
            e_ref = jnp.exp(ref)
            e_lr = jnp.exp(last - ref)
            lhs = jnp.concatenate([stack(kkd), stack(rd)], axis=0)
            gk = _dot_nt_hi(lhs, stack(kd))
            gb = _dot_nt_hi(lhs, stack(bd))
            a_kk = jnp.where(strict, gk[0:2 * c_len], 0.0)
            a_bk = jnp.where(strict, gb[0:2 * c_len], 0.0)
            b_rk = jnp.where(incl, gk[2 * c_len:], 0.0)
            b_rb = jnp.where(incl, gb[2 * c_len:], 0.0)
            inv = eye - a_bk
            pw = _dot_hi(a_bk, a_bk)
            n_fac = c_len.bit_length() - 2
            for it in range(n_fac):
                inv = inv + _dot_hi(pw, inv)
                if it < n_fac - 1:
                    pw = _dot_hi(pw, pw)
            n0 = st_ref[p]
            kr = _dot_nt_hi(jnp.concatenate([stack(kkd * e_ref), stack(rd * e_ref)], axis=0), n0)
            v_st = stack(v_s[rows, ln])
            sa = _dot_hi(inv, kr[0:2 * c_len] + _dot_hi(a_kk, v_st))
            y_st = kr[2 * c_len:] + _dot_hi(b_rk, v_st) - _dot_hi(b_rb, sa)
            y_s[rows, ln] = y_st[0:c_len] + y_st[c_len:]
            upd = lax.dot_general(v_st, stack(kd * e_lr), (((0,), (0,)), ((), ())), preferred_element_type=F32,
                                  precision=lax.Precision.HIGHEST)
            upd = upd - lax.dot_general(sa, stack(bd * e_lr), (((0,), (0,)), ((), ())), preferred_element_type=F32,
                                        precision=lax.Precision.HIGHEST)
            st_ref[p] = n0 * jnp.exp(last) + upd
        return carry

    lax.fori_loop(0, t_len // c_len, chunk, 0)

    y = y_s[...]
    mean = _split_dot(y, ones) * (1.0 / HEAD_DIM)
    yc = y - mean
    var = _split_dot(yc * yc, ones) * (1.0 / HEAD_DIM)
    y = yc * lax.rsqrt(var + RWKV_GN_EPS) * lnw_ref[...] + lnb_ref[...]
    o_ref[...] = ((y + bonus) * g).astype(o_ref.dtype)


def _rwkv(pd, mu, w0, w2p, a0, a2p, g2, k_k, k_a, r_k, ln_w, ln_b, tile=512):
    b, s, _ = pd.shape
    full = lambda a: pl.BlockSpec(a.shape, lambda bi, i: (0, 0))
    params = (mu, w0, w2p, a0, a2p, g2, k_k, k_a, r_k, ln_w, ln_b)
    sc = lambda: pltpu.VMEM((tile, GROUP_WIDTH), F32)
    return pl.pallas_call(
        _rwkv_kernel,
        grid=(b, s // tile),
        in_specs=[pl.BlockSpec((None, tile, D_COLS), lambda bi, i: (bi, i, 0))] + [full(a) for a in params],
        out_specs=pl.BlockSpec((None, tile, GROUP_WIDTH), lambda bi, i: (bi, i, 0)),
        out_shape=jax.ShapeDtypeStruct((b, s, GROUP_WIDTH), BF16),
        scratch_shapes=[pltpu.VMEM((2, LANES, LANES), F32), pltpu.VMEM((8, D_COLS), F32),
                        sc(), sc(), sc(), sc(), sc(), sc(), sc()],
        compiler_params=_params("parallel", "arbitrary"),
        name="rwkv7",
    )(pd, *params)


def _out_proj_kernel(x_ref, oa_ref, ob_ref, oc_ref, od_ref, w_ref, o_ref):
    gw = GROUP_WIDTH
    acc = x_ref[...]
    for i, m_ref in enumerate((oa_ref, ob_ref, oc_ref, od_ref)):
        acc = acc + _dot(m_ref[...], w_ref[i * gw:(i + 1) * gw, :])
    o_ref[...] = acc


def _out_proj(x2, oa, ob, oc, od, w_out, tm=512):
    n = x2.shape[0]
    row = lambda c: pl.BlockSpec((tm, c), lambda i: (i, 0))
    return pl.pallas_call(
        _out_proj_kernel,
        grid=(n // tm,),
        in_specs=[row(D_MODEL)] + [row(GROUP_WIDTH)] * 4 + [pl.BlockSpec(w_out.shape, lambda i: (0, 0))],
        out_specs=row(D_MODEL),
        out_shape=jax.ShapeDtypeStruct((n, D_MODEL), F32),
        compiler_params=_params("parallel"),
        name="out_proj",
    )(x2, oa, ob, oc, od, w_out)


def _swiglu_chunks(h, w1_ref, w3_ref, w2_ref, d_ff, chunk=256):
    out = None
    for lo in range(0, d_ff, chunk):
        hi = min(lo + chunk, d_ff)
        a = _dot(h, w1_ref[:, lo:hi])
        b = _dot(h, w3_ref[:, lo:hi])
        act = (a * _sigmoid(a) * b).astype(BF16)
        y = _dot(act, w2_ref[lo:hi, :])
        out = y if out is None else out + y
    return out


def _ffn_kernel(x_ref, g_ref, w1_ref, w3_ref, w2_ref, o_ref):
    x = x_ref[...]
    h = (_rms_scale(x) * g_ref[...]).astype(BF16)
    o_ref[...] = x + _swiglu_chunks(h, w1_ref, w3_ref, w2_ref, D_FF)


def _ffn(x2, g, w1, w3, w2, tm=512):
    n = x2.shape[0]
    row = pl.BlockSpec((tm, D_MODEL), lambda i: (i, 0))
    full = lambda a: pl.BlockSpec(a.shape, lambda i: (0, 0))
    return pl.pallas_call(
        _ffn_kernel,
        grid=(n // tm,),
        in_specs=[row, full(g), full(w1), full(w3), full(w2)],
        out_specs=row,
        out_shape=jax.ShapeDtypeStruct((n, D_MODEL), F32),
        compiler_params=_params("parallel"),
        name="ffn_dense",
    )(x2, g, w1, w3, w2)


def _moe_kernel(x_ref, g_ref, router_ref, w1_ref, w3_ref, w2_ref, fin_ref, o_ref, h_ref, comb_ref, acc_ref, *,
                final_norm):
    e = pl.program_id(1)
    lane = _iota((1, LANES), 1).astype(F32)

    @pl.when(e == 0)
    def _():
        hf = _rms_scale(x_ref[...]) * g_ref[...]
        h_ref[...] = hf.astype(BF16)
        logits = jnp.where(lane < N_EXPERTS, _dot_hi(hf, router_ref[...]), -jnp.inf)
        m1 = jnp.max(logits, axis=-1, keepdims=True)
        i1 = jnp.min(jnp.where(logits == m1, lane, float(LANES)), axis=-1, keepdims=True)
        rest = jnp.where(lane == i1, -jnp.inf, logits)
        m2 = jnp.max(rest, axis=-1, keepdims=True)
        i2 = jnp.min(jnp.where(rest == m2, lane, float(LANES)), axis=-1, keepdims=True)
        e2 = jnp.exp(m2 - m1)
        g1 = 1.0 / (1.0 + e2)
        comb_ref[...] = jnp.where(lane == i1, g1, 0.0) + jnp.where(lane == i2, e2 * g1, 0.0)
        acc_ref[...] = jnp.zeros_like(acc_ref)

    gate = jnp.sum(jnp.where(lane == e.astype(F32), comb_ref[...], 0.0), axis=-1, keepdims=True)
    acc_ref[...] += gate * _swiglu_chunks(h_ref[...], w1_ref, w3_ref, w2_ref, D_FF_EXPERT)

    @pl.when(e == N_EXPERTS - 1)
    def _():
        y = x_ref[...] + acc_ref[...]
        if final_norm:
            y = _rms_scale(y) * fin_ref[...]
        o_ref[...] = y


def _moe(x2, g, router_p, w1, w3, w2, fin, final_norm, tm=512):
    n = x2.shape[0]
    row = pl.BlockSpec((tm, D_MODEL), lambda i, e: (i, 0))
    full = lambda a: pl.BlockSpec(a.shape, lambda i, e: (0, 0))
    exp = lambda a: pl.BlockSpec((None,) + a.shape[1:], lambda i, e: (e, 0, 0))
    return pl.pallas_call(
        functools.partial(_moe_kernel, final_norm=final_norm),
        grid=(n // tm, N_EXPERTS),
        in_specs=[row, full(g), full(router_p), exp(w1), exp(w3), exp(w2), full(fin)],
        out_specs=row,
        out_shape=jax.ShapeDtypeStruct((n, D_MODEL), F32),
        scratch_shapes=[pltpu.VMEM((tm, D_MODEL), BF16), pltpu.VMEM((tm, LANES), F32), pltpu.VMEM((tm, D_MODEL), F32)],
        compiler_params=_params("parallel", "arbitrary"),
        name="moe",
    )(x2, g, router_p, w1, w3, w2, fin)


def _rope_tables(positions):
    inv_freq = ROPE_THETA ** (-jnp.arange(0, HEAD_DIM, 2, dtype=F32) / HEAD_DIM)
    ang = positions.astype(F32)[..., None] * inv_freq
    cos, sin = jnp.cos(ang), jnp.sin(ang)
    cos_t = jnp.concatenate([cos, cos, cos, cos], axis=-1)
    sin_t = jnp.concatenate([-sin, sin, -sin, sin], axis=-1)
    return cos_t, sin_t


def _row(v):
    return v.reshape(1, -1).astype(F32)


def kernel(x, positions, attn_norm, w_in, swa_sinks, swa_out_norm, gla_w_a2, gla_b_a, gla_out_norm, sb_out_norm, rwkv_mu, rwkv_w0, rwkv_w2, rwkv_a0, rwkv_a2, rwkv_g2, rwkv_k_k, rwkv_k_a, rwkv_r_k, rwkv_ln_w, rwkv_ln_b, w_out, ffn_norm, ffn_w1, ffn_w3, ffn_w2, router, moe_w1, moe_w3, moe_w2, final_norm):
    b, s, d = x.shape
    n = b * s
    depth = w_in.shape[0]
    cos_t, sin_t = _rope_tables(positions)
    x2 = x.reshape(n, d)
    for l in range(depth):
        w = w_in[l].astype(BF16)
        wa = w[:, :A_COLS]
        wb = jnp.pad(w[:, A_COLS:A_COLS + B_COLS], ((0, 0), (0, B_COLS_PAD - B_COLS)))
        wc = w[:, A_COLS + B_COLS:A_COLS + B_COLS + C_COLS]
        wd = w[:, A_COLS + B_COLS + C_COLS:]
        pa, pb, pc, pd = _in_proj(x2, _row(attn_norm[l]), wa, wb, wc, wd)

        oa = _swa(pa.reshape(b, s, A_COLS), cos_t, sin_t, swa_sinks[l].astype(F32), _row(swa_out_norm[l]))
        wa2 = jnp.pad(gla_w_a2[l], ((0, LANES - GLA_LOW_RANK), (0, 0))).astype(BF16)
        ob = _gla(pb.reshape(b, s, B_COLS_PAD), wa2, _row(gla_b_a[l]), _row(gla_out_norm[l]))
        oc = _sb(pc.reshape(b, s, C_COLS), _row(sb_out_norm[l]))
        lora = rwkv_w2.shape[1]
        w2p = jnp.pad(rwkv_w2[l], ((0, LANES - lora), (0, 0))).astype(BF16)
        a2p = jnp.pad(rwkv_a2[l], ((LANES - lora, 0), (0, 0))).astype(BF16)
        od = _rwkv(pd.reshape(b, s, D_COLS), _row(rwkv_mu[l]), _row(rwkv_w0[l]), w2p, _row(rwkv_a0[l]), a2p,
                   rwkv_g2[l].astype(BF16), _row(rwkv_k_k[l]), _row(rwkv_k_a[l]), _row(rwkv_r_k[l]),
                   _row(rwkv_ln_w[l]), _row(rwkv_ln_b[l]))

        gw = GROUP_WIDTH
        x2 = _out_proj(x2, oa.reshape(n, gw), ob.reshape(n, gw), oc.reshape(n, gw), od.reshape(n, gw),
                       w_out[l].astype(BF16))
        if l % 2 == 0:
            i = l // 2
            x2 = _ffn(x2, _row(ffn_norm[l]), ffn_w1[i].astype(BF16), ffn_w3[i].astype(BF16), ffn_w2[i].astype(BF16))
        else:
            i = l // 2
            router_p = jnp.pad(router[i], ((0, 0), (0, LANES - N_EXPERTS))).astype(F32)
            x2 = _moe(x2, _row(ffn_norm[l]), router_p, moe_w1[i].astype(BF16), moe_w3[i].astype(BF16),
                      moe_w2[i].astype(BF16), _row(final_norm), final_norm=(l == depth - 1))
    if depth % 2 == 1:
        raise NotImplementedError("final norm is fused into the last (MoE) layer")
    return x2.reshape(b, s, d)
```

```python
import functools

import jax
import jax.numpy as jnp
from jax import lax
from jax.experimental import pallas as pl
from jax.experimental.pallas import tpu as pltpu

F32 = jnp.float32
BF16 = jnp.bfloat16

D_MODEL = 1024
HEAD_DIM = 64
GROUP_WIDTH = 256
SWA_WINDOW = 128
ROPE_THETA = 10000.0
GLA_HEADS = 4
GLA_DK = 32
GLA_LOW_RANK = 16
GLA_TAU = 16.0
GLA_CHUNK = 64
SB_BLOCK = 128
RWKV_CHUNK = 64
RWKV_DECAY_SCALE = 0.6065306597126334
RWKV_GN_EPS = 64e-5
D_FF = 2816
N_EXPERTS = 8
D_FF_EXPERT = 1408
NORM_EPS = 1e-6

A_COLS = 512
B_COLS = 784
C_COLS = 768
D_COLS = 1024
B_COLS_PAD = 896

LANES = 128
VMEM_LIMIT_BYTES = 56 * 1024 * 1024


def _params(*sem):
    return pltpu.CompilerParams(dimension_semantics=sem, vmem_limit_bytes=VMEM_LIMIT_BYTES)


def _dot(a, b):
    return jnp.dot(a, b, preferred_element_type=F32)


def _dot_nt(a, b):
    return lax.dot_general(a, b, (((1,), (1,)), ((), ())), preferred_element_type=F32)


def _dot_tn(a, b):
    return lax.dot_general(a, b, (((0,), (0,)), ((), ())), preferred_element_type=F32)


def _dot_hi(a, b):
    return jnp.dot(a, b, preferred_element_type=F32, precision=lax.Precision.HIGHEST)


def _dot_nt_hi(a, b):
    return lax.dot_general(a, b, (((1,), (1,)), ((), ())), preferred_element_type=F32,
                           precision=lax.Precision.HIGHEST)


def _inv_dot(a, b):
    return _dot(a.astype(BF16), b.astype(BF16))


def _split_dot(a, b_exact):
    hi = a.astype(BF16)
    lo = (a - hi.astype(F32)).astype(BF16)
    return _dot(hi, b_exact) + _dot(lo, b_exact)


def _split_dot_tri(tri, a):
    hi = a.astype(BF16)
    lo = (a - hi.astype(F32)).astype(BF16)
    return _dot(tri, hi) + _dot(tri, lo)


def _iota(shape, dim):
    return lax.broadcasted_iota(jnp.int32, shape, dim)


def _idiv(x, d):
    return jnp.right_shift(x, d.bit_length() - 1)


def _imod(x, d):
    return jnp.bitwise_and(x, d - 1)


def _group_ones(width, group):
    r = _idiv(_iota((width, width), 0), group)
    c = _idiv(_iota((width, width), 1), group)
    return jnp.where(r == c, 1.0, 0.0).astype(BF16)


def _softplus(z):
    return jnp.maximum(z, 0.0) + jnp.log1p(jnp.exp(-jnp.abs(z)))


def _sigmoid(z):
    return 1.0 / (1.0 + jnp.exp(-z))


def _rms_scale(x):
    return x * lax.rsqrt(jnp.mean(x * x, axis=-1, keepdims=True) + NORM_EPS)


def _in_proj_kernel(x_ref, g_ref, wa_ref, wb_ref, wc_ref, wd_ref, pa_ref, pb_ref, pc_ref, pd_ref):
    h = (_rms_scale(x_ref[...]) * g_ref[...]).astype(BF16)
    pa_ref[...] = _dot(h, wa_ref[...])
    pb_ref[...] = _dot(h, wb_ref[...])
    pc_ref[...] = _dot(h, wc_ref[...]).astype(BF16)
    pd_ref[...] = _dot(h, wd_ref[...])


def _in_proj(x2, g, wa, wb, wc, wd, tm=512):
    n = x2.shape[0]
    full = lambda w: pl.BlockSpec(w.shape, lambda i: (0, 0))
    row = lambda c: pl.BlockSpec((tm, c), lambda i: (i, 0))
    return pl.pallas_call(
        _in_proj_kernel,
        grid=(n // tm,),
        in_specs=[row(D_MODEL), full(g), full(wa), full(wb), full(wc), full(wd)],
        out_specs=[row(A_COLS), row(B_COLS_PAD), row(C_COLS), row(D_COLS)],
        out_shape=[jax.ShapeDtypeStruct((n, A_COLS), F32), jax.ShapeDtypeStruct((n, B_COLS_PAD), F32),
                   jax.ShapeDtypeStruct((n, C_COLS), BF16), jax.ShapeDtypeStruct((n, D_COLS), F32)],
        compiler_params=_params("parallel"),
        name="in_proj",
    )(x2, g, wa, wb, wc, wd)


def _swa_kernel(sink_ref, cur_ref, prev_ref, cc_ref, sc_ref, cp_ref, sp_ref, gain_ref, o_ref):
    w = SWA_WINDOW
    blk = pl.program_id(1)
    lane = _iota((1, LANES), 1)
    first_half = _imod(lane, HEAD_DIM) < (HEAD_DIM // 2)
    low_head = lane < HEAD_DIM

    def rope(x, c, s):
        rot = jnp.where(first_half, pltpu.roll(x, LANES - HEAD_DIM // 2, 1), pltpu.roll(x, HEAD_DIM // 2, 1))
        return x * c + rot * s

    cc, sc = cc_ref[...], sc_ref[...]
    scale = HEAD_DIM ** -0.5
    q01 = rope(cur_ref[:, 0:128], cc, sc) * scale
    q23 = rope(cur_ref[:, 128:256], cc, sc) * scale
    k_cur = rope(cur_ref[:, 256:384], cc, sc)
    k_prev = rope(prev_ref[:, 256:384], cp_ref[...], sp_ref[...])
    kw = jnp.concatenate([k_prev, k_cur], axis=0).astype(BF16)
    vw = jnp.concatenate([prev_ref[:, 384:512], cur_ref[:, 384:512]], axis=0).astype(BF16)
    qs = jnp.concatenate([
        jnp.where(low_head, q01, 0.0),
        jnp.where(low_head, pltpu.roll(q01, HEAD_DIM, 1), 0.0),
        jnp.where(low_head, 0.0, pltpu.roll(q23, HEAD_DIM, 1)),
        jnp.where(low_head, 0.0, q23)], axis=0).astype(BF16)
    s_all = _dot_nt(qs, kw)

    qi = _iota((w, 2 * w), 0)
    kj = _iota((w, 2 * w), 1)
    valid = (kj > qi) & (kj <= qi + w) & ((blk > 0) | (kj >= w))
    heads = []
    for h in range(4):
        s = s_all[h * w:(h + 1) * w]
        sink = sink_ref[h]
        m = jnp.maximum(jnp.max(jnp.where(valid, s, -1e30), axis=-1, keepdims=True), sink)
        p = jnp.where(valid, jnp.exp(s - m), 0.0)
        den = jnp.sum(p, axis=-1, keepdims=True) + jnp.exp(sink - m)
        o = _dot(p.astype(BF16), vw) / den
        own = low_head if h < 2 else jnp.logical_not(low_head)
        ms = jnp.sum(jnp.where(own, o * o, 0.0), axis=-1, keepdims=True) * (1.0 / HEAD_DIM)
        heads.append(o * lax.rsqrt(ms + NORM_EPS))
    out01 = jnp.where(low_head, heads[0], pltpu.roll(heads[1], HEAD_DIM, 1))
    out23 = jnp.where(low_head, pltpu.roll(heads[2], HEAD_DIM, 1), heads[3])
    o_ref[...] = (jnp.concatenate([out01, out23], axis=1) * gain_ref[...]).astype(o_ref.dtype)


def _swa(pa, cos_t, sin_t, sinks, gain):
    b, s, _ = pa.shape
    w = SWA_WINDOW
    cur = lambda c: pl.BlockSpec((None, w, c), lambda bi, i: (bi, i, 0))
    prev = lambda c: pl.BlockSpec((None, w, c), lambda bi, i: (bi, jnp.maximum(i - 1, 0), 0))
    return pl.pallas_call(
        _swa_kernel,
        grid=(b, s // w),
        in_specs=[pl.BlockSpec(memory_space=pltpu.SMEM), cur(A_COLS), prev(A_COLS),
                  cur(LANES), cur(LANES), prev(LANES), prev(LANES),
                  pl.BlockSpec((1, GROUP_WIDTH), lambda bi, i: (0, 0))],
        out_specs=cur(GROUP_WIDTH),
        out_shape=jax.ShapeDtypeStruct((b, s, GROUP_WIDTH), BF16),
        compiler_params=_params("parallel", "parallel"),
        name="swa",
    )(sinks, pa, pa, cos_t, sin_t, cos_t, sin_t, gain)


def _gla_kernel(pb_ref, wa2_ref, ba_ref, gain_ref, o_ref, st_ref, ob_ref, la_ref):
    c_len = GLA_CHUNK
    t_len = pb_ref.shape[0]
    nh = GLA_HEADS

    @pl.when(pl.program_id(1) == 0)
    def _():
        st_ref[...] = jnp.zeros_like(st_ref)

    lr = pb_ref[:, 768:896].astype(BF16)
    la_ref[...] = -_softplus(-(_dot(lr, wa2_ref[...]) + ba_ref[...])) * (1.0 / GLA_TAU)

    tri = jnp.where(_iota((c_len, c_len), 0) >= _iota((c_len, c_len), 1), 1.0, 0.0).astype(BF16)
    klane = _idiv(_iota((1, LANES), 1), GLA_DK)
    vlane = _idiv(_iota((1, GROUP_WIDTH), 1), HEAD_DIM)
    r_i = _iota((nh * c_len, nh * c_len), 0)
    c_i = _iota((nh * c_len, nh * c_len), 1)
    causal = (_idiv(r_i, c_len) == _idiv(c_i, c_len)) & (_imod(r_i, c_len) >= _imod(c_i, c_len))
    st_mask = _idiv(_iota((GROUP_WIDTH, LANES), 0), HEAD_DIM) == _idiv(_iota((GROUP_WIDTH, LANES), 1), GLA_DK)

    def stack_k(x):
        return jnp.concatenate([jnp.where(klane == h, x, 0.0) for h in range(nh)], axis=0).astype(BF16)

    def chunk(ci, carry):
        rows = pl.ds(pl.multiple_of(ci * c_len, c_len), c_len)
        b = _split_dot_tri(tri, la_ref[rows, :])
        b_ref = b[c_len // 2 - 1:c_len // 2]
        b_last = b[c_len - 1:c_len]
        q = pb_ref[rows, 0:128] * (GLA_DK ** -0.5)
        k = pb_ref[rows, 128:256]
        v = pb_ref[rows, 256:512]
        att = _dot_nt(stack_k(q * jnp.exp(b - b_ref)), stack_k(k * jnp.exp(b_ref - b)))
        att = jnp.where(causal, att, 0.0).astype(BF16)
        v_st = jnp.concatenate([jnp.where(vlane == h, v, 0.0) for h in range(nh)], axis=0).astype(BF16)
        st = st_ref[...]
        o_st = _dot(att, v_st) + _dot_nt(stack_k(q * jnp.exp(b)), st.astype(BF16))
        o = o_st[0:c_len]
        for h in range(1, nh):
            o = o + o_st[h * c_len:(h + 1) * c_len]
        ob_ref[rows, :] = o
        kv = _dot_tn(v.astype(BF16), (k * jnp.exp(b_last - b)).astype(BF16))
        st_ref[...] = st * jnp.exp(b_last) + jnp.where(st_mask, kv, 0.0)
        return carry

    lax.fori_loop(0, t_len // c_len, chunk, 0)

    ob = ob_ref[...]
    ms = _split_dot(ob * ob, _group_ones(GROUP_WIDTH, HEAD_DIM)) * (1.0 / HEAD_DIM)
    g = pb_ref[:, 512:768]
    y = ob * lax.rsqrt(ms + NORM_EPS) * gain_ref[...] * (g * _sigmoid(g))
    o_ref[...] = y.astype(o_ref.dtype)


def _gla(pb, wa2, ba, gain, tile=512):
    b, s, _ = pb.shape
    full = lambda a: pl.BlockSpec(a.shape, lambda bi, i: (0, 0))
    return pl.pallas_call(
        _gla_kernel,
        grid=(b, s // tile),
        in_specs=[pl.BlockSpec((None, tile, B_COLS_PAD), lambda bi, i: (bi, i, 0)), full(wa2), full(ba), full(gain)],
        out_specs=pl.BlockSpec((None, tile, GROUP_WIDTH), lambda bi, i: (bi, i, 0)),
        out_shape=jax.ShapeDtypeStruct((b, s, GROUP_WIDTH), BF16),
        scratch_shapes=[pltpu.VMEM((GROUP_WIDTH, LANES), F32), pltpu.VMEM((tile, GROUP_WIDTH), F32),
                        pltpu.VMEM((tile, LANES), F32)],
        compiler_params=_params("parallel", "arbitrary"),
        name="gla",
    )(pb, wa2, ba, gain)


def _sb_kernel(pc_ref, gain_ref, o_ref, kcat_ref, vcat_ref, *, tq):
    t = SB_BLOCK
    nh = 4
    nq = tq // t
    qb = pl.program_id(1)
    nblk = pc_ref.shape[0] // t
    hlane = _idiv(_iota((1, GROUP_WIDTH), 1), HEAD_DIM)

    @pl.when(qb == 0)
    def _():
        def fill(kb, carry):
            rows = pl.ds(pl.multiple_of(kb * t, t), t)
            k = pc_ref[rows, 256:512]
            v = pc_ref[rows, 512:768]
            for h in range(nh):
                kcat_ref[kb, h * t:(h + 1) * t, :] = jnp.where(hlane == h, k, jnp.zeros_like(k))
                vcat_ref[kb, h * t:(h + 1) * t, :] = jnp.where(hlane == h, v, jnp.zeros_like(v))
            return carry
        lax.fori_loop(0, nblk, fill, 0)

    q = pc_ref[pl.ds(pl.multiple_of(qb * tq, tq), tq), 0:256] * (HEAD_DIM ** -0.5)
    uj = jnp.where((_iota((t, 2 * t), 0) > _iota((t, 2 * t), 1)) | (_iota((t, 2 * t), 1) >= t), 1.0, 0.0).astype(BF16)
    uj2 = jnp.concatenate([uj, uj], axis=0)
    key_in_blk = _imod(_iota((tq, nh * t), 1), t)
    row_in_tile = _iota((tq, nh * t), 0)

    def scores(kb):
        return _dot_nt(q, kcat_ref[kb])

    def block(kb, z, c, acc, strict):
        sp = jnp.maximum(z, 0.0) + jnp.log(1.0 + jnp.exp(-jnp.abs(z)))
        if strict is not None:
            sp = jnp.where(strict, sp, 0.0)
        args, c_new = [], []
        for h in range(nh):
            sl = slice(h * t, (h + 1) * t)
            sp_hi = sp[:, sl].astype(BF16)
            sp_lo = (sp[:, sl] - sp_hi.astype(F32)).astype(BF16)
            cj = _dot(jnp.concatenate([sp_hi, sp_lo], axis=1), uj2)
            args.append(z[:, sl] - sp[:, sl] - cj[:, :t] - c[h])
            c_new.append(c[h] + cj[:, t:])
        attn = jnp.exp(jnp.concatenate(args, axis=1))
        if strict is not None:
            attn = jnp.where(strict, attn, 0.0)
        return tuple(c_new), attn.astype(BF16)

    c = (jnp.zeros((tq, t), F32),) * nh
    acc = jnp.zeros((tq, GROUP_WIDTH), F32)
    attn = None
    for d in range(nq - 1, -1, -1):
        kb = qb * nq + d
        z = scores(kb)
        if attn is not None:
            acc = acc + _dot(attn, vcat_ref[kb + 1])
        c, attn = block(kb, z, c, acc, key_in_blk + d * t < row_in_tile)

    def body(i, carry):
        c, acc, attn = carry
        kb = qb * nq - 1 - i
        z = scores(kb)
        acc = acc + _dot(attn, vcat_ref[kb + 1])
        c, attn = block(kb, z, c, acc, None)
        return c, acc, attn

    c, acc, attn = lax.fori_loop(0, qb * nq, body, (c, acc, attn))
    acc = acc + _dot(attn, vcat_ref[0])
    ms = _split_dot(acc * acc, _group_ones(GROUP_WIDTH, HEAD_DIM)) * (1.0 / HEAD_DIM)
    o_ref[...] = (acc * lax.rsqrt(ms + NORM_EPS) * gain_ref[...]).astype(o_ref.dtype)


def _sb(pc, gain, tq=512):
    b, s, _ = pc.shape
    t = SB_BLOCK
    return pl.pallas_call(
        functools.partial(_sb_kernel, tq=tq),
        grid=(b, s // tq),
        in_specs=[pl.BlockSpec((None, s, C_COLS), lambda bi, i: (bi, 0, 0)),
                  pl.BlockSpec((1, GROUP_WIDTH), lambda bi, i: (0, 0))],
        out_specs=pl.BlockSpec((None, tq, GROUP_WIDTH), lambda bi, i: (bi, i, 0)),
        out_shape=jax.ShapeDtypeStruct((b, s, GROUP_WIDTH), BF16),
        scratch_shapes=[pltpu.VMEM((s // t, 4 * t, GROUP_WIDTH), BF16), pltpu.VMEM((s // t, 4 * t, GROUP_WIDTH), BF16)],
        compiler_params=_params("parallel", "arbitrary"),
        name="stick_breaking",
    )(pc, gain)


def _rwkv_kernel(pd_ref, mu_ref, w0_ref, w2_ref, a0_ref, a2_ref, g2_ref, kk_ref, ka_ref, rk_ref, lnw_ref, lnb_ref,
                 o_ref, st_ref, carry_ref, r_s, k_s, v_s, kk_s, b_s, lw_s, y_s,
                 l2_s, w_s, y0_s, brb_s, vk_s, bl_s, el_s):
    c_len = RWKV_CHUNK
    t_len = pd_ref.shape[0]
    gw = GROUP_WIDTH

    @pl.when(pl.program_id(1) == 0)
    def _():
        st_ref[...] = jnp.zeros_like(st_ref)
        carry_ref[...] = jnp.zeros_like(carry_ref)

    cur = pd_ref[...]
    shifted = jnp.where(_iota((t_len, 1), 0) == 0, carry_ref[7:8, :], pltpu.roll(cur, 1, 0))
    carry_ref[...] = cur[t_len - 8:t_len, :]
    xm = cur + (shifted - cur) * mu_ref[...]
    r = xm[:, 0:gw]
    k = xm[:, gw:2 * gw]
    v = xm[:, 2 * gw:3 * gw]
    wa = xm[:, 3 * gw:3 * gw + 128]
    gl = xm[:, 3 * gw + 128:4 * gw]
    lora_lo = _iota((1, LANES), 1) < 64
    wl = jnp.where(lora_lo, jnp.tanh(wa), 0.0).astype(BF16)
    lw_s[...] = -RWKV_DECAY_SCALE * _sigmoid(w0_ref[...] + _dot(wl, w2_ref[...]))
    a = _sigmoid(a0_ref[...] + _dot(wa.astype(BF16), a2_ref[...]))
    g = _dot(_sigmoid(gl).astype(BF16), g2_ref[...])
    ones = _group_ones(gw, HEAD_DIM)
    kk = k * kk_ref[...]
    kk = kk * lax.rsqrt(_split_dot(kk * kk, ones) + 1e-12)
    k2 = k * (1.0 + (a - 1.0) * ka_ref[...])
    r_s[...] = r
    k_s[...] = k2
    v_s[...] = v
    kk_s[...] = kk
    b_s[...] = kk * a
    bonus = _split_dot(r * k2 * rk_ref[...], ones) * v

    tri = jnp.where(_iota((c_len, c_len), 0) >= _iota((c_len, c_len), 1), 1.0, 0.0).astype(BF16)
    head0 = _iota((1, LANES), 1) < HEAD_DIM
    r_i = _iota((2 * c_len, 2 * c_len), 0)
    c_i = _iota((2 * c_len, 2 * c_len), 1)
    same = _idiv(r_i, c_len) == _idiv(c_i, c_len)
    strict = same & (_imod(r_i, c_len) > _imod(c_i, c_len))
    incl = same & (_imod(r_i, c_len) >= _imod(c_i, c_len))
    eye = jnp.where(r_i == c_i, 1.0, 0.0)

    def stack(x):
        return jnp.concatenate([jnp.where(head0, x, 0.0), jnp.where(head0, 0.0, x)], axis=0)

    b16 = lambda x: x.astype(BF16)
    two_c = 2 * c_len

    def load_chunk(ci, p):
        rows = pl.ds(pl.multiple_of(ci * c_len, c_len), c_len)
        ln = slice(p * LANES, (p + 1) * LANES)
        return tuple(s[rows, ln] for s in (lw_s, r_s, kk_s, k_s, b_s, v_s))

    def prepare(chains):
        n = range(len(chains))
        lw, r_c, kk_c, k_c, b_c, v_c = zip(*chains)
        cum = [_split_dot_tri(tri, lw[i]) for i in n]
        ref = [cum[i][c_len // 2 - 1:c_len // 2] for i in n]
        last = [cum[i][c_len - 1:c_len] for i in n]
        e_neg = [jnp.exp(ref[i] - cum[i]) for i in n]
        rd = [r_c[i] * jnp.exp(cum[i] - ref[i]) for i in n]
        kkd = [kk_c[i] * jnp.exp(cum[i] - lw[i] - ref[i]) for i in n]
        kd = [k_c[i] * e_neg[i] for i in n]
        bd = [b_c[i] * e_neg[i] for i in n]
        e_ref = [jnp.exp(ref[i]) for i in n]
        e_lr = [jnp.exp(last[i] - ref[i]) for i in n]
        lhs = [b16(jnp.concatenate([stack(kkd[i]), stack(rd[i])], axis=0)) for i in n]
        gk = [_dot_nt(lhs[i], b16(stack(kd[i]))) for i in n]
        gb = [_dot_nt(lhs[i], b16(stack(bd[i]))) for i in n]
        a_kk = [b16(jnp.where(strict, gk[i][0:two_c], 0.0)) for i in n]
        a_bk = [jnp.where(strict, gb[i][0:two_c], 0.0) for i in n]
        b_rk = [b16(jnp.where(incl, gk[i][two_c:], 0.0)) for i in n]
        b_rb = [b16(jnp.where(incl, gb[i][two_c:], 0.0)) for i in n]
        v_st = [b16(stack(v_c[i])) for i in n]
        akv = [b16(_dot(a_kk[i], v_st[i])) for i in n]
        y0 = [_dot(b_rk[i], v_st[i]) for i in n]
        vk = [_dot_tn(v_st[i], b16(stack(kd[i] * e_lr[i]))) for i in n]
        inv = [eye - a_bk[i] for i in n]
        pw = [_inv_dot(a_bk[i], a_bk[i]) for i in n]
        n_fac = c_len.bit_length() - 2
        for it in range(n_fac):
            inv = [inv[i] + _inv_dot(pw[i], inv[i]) for i in n]
            if it < n_fac - 1:
                pw = [_inv_dot(pw[i], pw[i]) for i in n]
        inv_b = [b16(inv[i]) for i in n]
        w = [_dot(inv_b[i], akv[i]) for i in n]
        u = [b16(_dot(inv_b[i], b16(stack(kkd[i] * e_ref[i])))) for i in n]
        return [(jnp.concatenate([u[i], b16(stack(rd[i] * e_ref[i]))], axis=0),
                 w[i], y0[i], b_rb[i], vk[i], b16(stack(bd[i] * e_lr[i])),
                 jnp.broadcast_to(jnp.exp(last[i]), (8, LANES))) for i in n]

    prep_refs = (l2_s, w_s, y0_s, brb_s, vk_s, bl_s, el_s)
    group = 4

    def prepare_group(i, carry):
        idx = [(group * i + dc, p) for dc in range(group) for p in range(2)]
        outs = prepare([load_chunk(ci, p) for ci, p in idx])
        for (ci, p), out in zip(idx, outs):
            for ref, val in zip(prep_refs, out):
                ref[ci * 2 + p] = val
        return carry

    lax.fori_loop(0, t_len // (group * c_len), prepare_group, 0)

    def advance(ci, carry):
        rows = pl.ds(pl.multiple_of(ci * c_len, c_len), c_len)
        pr = range(2)
        n0 = [st_ref[p] for p in pr]
        l2, w, y0, brb, vk, bl, el = zip(*[tuple(ref[ci * 2 + p] for ref in prep_refs) for p in pr])
        kr = [_dot_nt(l2[p], b16(n0[p])) for p in pr]
        sa_b = [b16(kr[p][0:two_c] + w[p]) for p in pr]
        y_st = [kr[p][two_c:] + y0[p] - _dot(brb[p], sa_b[p]) for p in pr]
        n1 = [n0[p] * el[p][0:1, :] + vk[p] - _dot_tn(sa_b[p], bl[p]) for p in pr]
        for p in pr:
            y_s[rows, p * LANES:(p + 1) * LANES] = y_st[p][0:c_len] + y_st[p][c_len:]
            st_ref[p] = n1[p]
        return carry

    lax.fori_loop(0, t_len // c_len, advance, 0)

    y = y_s[...]
    mean = _split_dot(y, ones) * (1.0 / HEAD_DIM)
    yc = y - mean
    var = _split_dot(yc * yc, ones) * (1.0 / HEAD_DIM)
    y = yc * lax.rsqrt(var + RWKV_GN_EPS) * lnw_ref[...] + lnb_ref[...]
    o_ref[...] = ((y + bonus) * g).astype(o_ref.dtype)


def _rwkv(pd, mu, w0, w2p, a0, a2p, g2, k_k, k_a, r_k, ln_w, ln_b, tile=512):
    b, s, _ = pd.shape
    full = lambda a: pl.BlockSpec(a.shape, lambda bi, i: (0, 0))
    params = (mu, w0, w2p, a0, a2p, g2, k_k, k_a, r_k, ln_w, ln_b)
    sc = lambda: pltpu.VMEM((tile, GROUP_WIDTH), F32)
    c = RWKV_CHUNK
    nj = 2 * (tile // c)
    return pl.pallas_call(
        _rwkv_kernel,
        grid=(b, s // tile),
        in_specs=[pl.BlockSpec((None, tile, D_COLS), lambda bi, i: (bi, i, 0))] + [full(a) for a in params],
        out_specs=pl.BlockSpec((None, tile, GROUP_WIDTH), lambda bi, i: (bi, i, 0)),
        out_shape=jax.ShapeDtypeStruct((b, s, GROUP_WIDTH), BF16),
        scratch_shapes=[pltpu.VMEM((2, LANES, LANES), F32), pltpu.VMEM((8, D_COLS), F32),
                        sc(), sc(), sc(), sc(), sc(), sc(), sc(),
                        pltpu.VMEM((nj, 4 * c, LANES), BF16),
                        pltpu.VMEM((nj, 2 * c, LANES), F32), pltpu.VMEM((nj, 2 * c, LANES), F32),
                        pltpu.VMEM((nj, 2 * c, 2 * c), BF16), pltpu.VMEM((nj, LANES, LANES), F32),
                        pltpu.VMEM((nj, 2 * c, LANES), BF16), pltpu.VMEM((nj, 8, LANES), F32)],
        compiler_params=_params("parallel", "arbitrary"),
        name="rwkv7",
    )(pd, *params)


def _out_proj_kernel(x_ref, oa_ref, ob_ref, oc_ref, od_ref, w_ref, o_ref):
    gw = GROUP_WIDTH
    acc = x_ref[...]
    for i, m_ref in enumerate((oa_ref, ob_ref, oc_ref, od_ref)):
        acc = acc + _dot(m_ref[...], w_ref[i * gw:(i + 1) * gw, :])
    o_ref[...] = acc


def _out_proj(x2, oa, ob, oc, od, w_out, tm=512):
    n = x2.shape[0]
    row = lambda c: pl.BlockSpec((tm, c), lambda i: (i, 0))
    return pl.pallas_call(
        _out_proj_kernel,
        grid=(n // tm,),
        in_specs=[row(D_MODEL)] + [row(GROUP_WIDTH)] * 4 + [pl.BlockSpec(w_out.shape, lambda i: (0, 0))],
        out_specs=row(D_MODEL),
        out_shape=jax.ShapeDtypeStruct((n, D_MODEL), F32),
        compiler_params=_params("parallel"),
        name="out_proj",
    )(x2, oa, ob, oc, od, w_out)


def _swiglu_chunks(h, w1_ref, w3_ref, w2_ref, d_ff, chunk=256):
    out = None
    for lo in range(0, d_ff, chunk):
        hi = min(lo + chunk, d_ff)
        a = _dot(h, w1_ref[:, lo:hi])
        b = _dot(h, w3_ref[:, lo:hi])
        act = (a * _sigmoid(a) * b).astype(BF16)
        y = _dot(act, w2_ref[lo:hi, :])
        out = y if out is None else out + y
    return out


def _ffn_kernel(x_ref, g_ref, w1_ref, w3_ref, w2_ref, o_ref):
    x = x_ref[...]
    h = (_rms_scale(x) * g_ref[...]).astype(BF16)
    o_ref[...] = x + _swiglu_chunks(h, w1_ref, w3_ref, w2_ref, D_FF)


def _ffn(x2, g, w1, w3, w2, tm=512):
    n = x2.shape[0]
    row = pl.BlockSpec((tm, D_MODEL), lambda i: (i, 0))
    full = lambda a: pl.BlockSpec(a.shape, lambda i: (0, 0))
    return pl.pallas_call(
        _ffn_kernel,
        grid=(n // tm,),
        in_specs=[row, full(g), full(w1), full(w3), full(w2)],
        out_specs=row,
        out_shape=jax.ShapeDtypeStruct((n, D_MODEL), F32),
        compiler_params=_params("parallel"),
        name="ffn_dense",
    )(x2, g, w1, w3, w2)


def _moe_kernel(x_ref, g_ref, router_ref, w1_ref, w3_ref, w2_ref, fin_ref, o_ref, h_ref, comb_ref, acc_ref, *,
                final_norm):
    e = pl.program_id(1)
    lane = _iota((1, LANES), 1).astype(F32)

    @pl.when(e == 0)
    def _():
        hf = _rms_scale(x_ref[...]) * g_ref[...]
        h_ref[...] = hf.astype(BF16)
        logits = jnp.where(lane < N_EXPERTS, _dot_hi(hf, router_ref[...]), -jnp.inf)
        m1 = jnp.max(logits, axis=-1, keepdims=True)
        i1 = jnp.min(jnp.where(logits == m1, lane, float(LANES)), axis=-1, keepdims=True)
        rest = jnp.where(lane == i1, -jnp.inf, logits)
        m2 = jnp.max(rest, axis=-1, keepdims=True)
        i2 = jnp.min(jnp.where(rest == m2, lane, float(LANES)), axis=-1, keepdims=True)
        e2 = jnp.exp(m2 - m1)
        g1 = 1.0 / (1.0 + e2)
        comb_ref[...] = jnp.where(lane == i1, g1, 0.0) + jnp.where(lane == i2, e2 * g1, 0.0)
        acc_ref[...] = jnp.zeros_like(acc_ref)

    gate = jnp.sum(jnp.where(lane == e.astype(F32), comb_ref[...], 0.0), axis=-1, keepdims=True)
    acc_ref[...] += gate * _swiglu_chunks(h_ref[...], w1_ref, w3_ref, w2_ref, D_FF_EXPERT)

    @pl.when(e == N_EXPERTS - 1)
    def _():
        y = x_ref[...] + acc_ref[...]
        if final_norm:
            y = _rms_scale(y) * fin_ref[...]
        o_ref[...] = y


def _moe(x2, g, router_p, w1, w3, w2, fin, final_norm, tm=1024):
    n = x2.shape[0]
    row = pl.BlockSpec((tm, D_MODEL), lambda i, e: (i, 0))
    full = lambda a: pl.BlockSpec(a.shape, lambda i, e: (0, 0))
    exp = lambda a: pl.BlockSpec((None,) + a.shape[1:], lambda i, e: (e, 0, 0))
    return pl.pallas_call(
        functools.partial(_moe_kernel, final_norm=final_norm),
        grid=(n // tm, N_EXPERTS),
        in_specs=[row, full(g), full(router_p), exp(w1), exp(w3), exp(w2), full(fin)],
        out_specs=row,
        out_shape=jax.ShapeDtypeStruct((n, D_MODEL), F32),
        scratch_shapes=[pltpu.VMEM((tm, D_MODEL), BF16), pltpu.VMEM((tm, LANES), F32), pltpu.VMEM((tm, D_MODEL), F32)],
        compiler_params=_params("parallel", "arbitrary"),
        name="moe",
    )(x2, g, router_p, w1, w3, w2, fin)


def _rope_tables(positions):
    inv_freq = ROPE_THETA ** (-jnp.arange(0, HEAD_DIM, 2, dtype=F32) / HEAD_DIM)
    ang = positions.astype(F32)[..., None] * inv_freq
    cos, sin = jnp.cos(ang), jnp.sin(ang)
    cos_t = jnp.concatenate([cos, cos, cos, cos], axis=-1)
    sin_t = jnp.concatenate([-sin, sin, -sin, sin], axis=-1)
    return cos_t, sin_t


def _row(v):
    return v.reshape(1, -1).astype(F32)


def kernel(x, positions, attn_norm, w_in, swa_sinks, swa_out_norm, gla_w_a2, gla_b_a, gla_out_norm, sb_out_norm, rwkv_mu, rwkv_w0, rwkv_w2, rwkv_a0, rwkv_a2, rwkv_g2, rwkv_k_k, rwkv_k_a, rwkv_r_k, rwkv_ln_w, rwkv_ln_b, w_out, ffn_norm, ffn_w1, ffn_w3, ffn_w2, router, moe_w1, moe_w3, moe_w2, final_norm):
    b, s, d = x.shape
    n = b * s
    depth = w_in.shape[0]
    cos_t, sin_t = _rope_tables(positions)
    x2 = x.reshape(n, d)
    for l in range(depth):
        w = w_in[l].astype(BF16)
        wa = w[:, :A_COLS]
        wb = jnp.pad(w[:, A_COLS:A_COLS + B_COLS], ((0, 0), (0, B_COLS_PAD - B_COLS)))
        wc = w[:, A_COLS + B_COLS:A_COLS + B_COLS + C_COLS]
        wd = w[:, A_COLS + B_COLS + C_COLS:]
        pa, pb, pc, pd = _in_proj(x2, _row(attn_norm[l]), wa, wb, wc, wd)

        oa = _swa(pa.reshape(b, s, A_COLS), cos_t, sin_t, swa_sinks[l].astype(F32), _row(swa_out_norm[l]))
        wa2 = jnp.pad(gla_w_a2[l], ((0, LANES - GLA_LOW_RANK), (0, 0))).astype(BF16)
        ob = _gla(pb.reshape(b, s, B_COLS_PAD), wa2, _row(gla_b_a[l]), _row(gla_out_norm[l]))
        oc = _sb(pc.reshape(b, s, C_COLS), _row(sb_out_norm[l]))
        lora = rwkv_w2.shape[1]
        w2p = jnp.pad(rwkv_w2[l], ((0, LANES - lora), (0, 0))).astype(BF16)
        a2p = jnp.pad(rwkv_a2[l], ((LANES - lora, 0), (0, 0))).astype(BF16)
        od = _rwkv(pd.reshape(b, s, D_COLS), _row(rwkv_mu[l]), _row(rwkv_w0[l]), w2p, _row(rwkv_a0[l]), a2p,
                   rwkv_g2[l].astype(BF16), _row(rwkv_k_k[l]), _row(rwkv_k_a[l]), _row(rwkv_r_k[l]),
                   _row(rwkv_ln_w[l]), _row(rwkv_ln_b[l]))

        gw = GROUP_WIDTH
        x2 = _out_proj(x2, oa.reshape(n, gw), ob.reshape(n, gw), oc.reshape(n, gw), od.reshape(n, gw),
                       w_out[l].astype(BF16))
        if l % 2 == 0:
            i = l // 2
            x2 = _ffn(x2, _row(ffn_norm[l]), ffn_w1[i].astype(BF16), ffn_w3[i].astype(BF16), ffn_w2[i].astype(BF16))
        else:
            i = l // 2
            router_p = jnp.pad(router[i], ((0, 0), (0, LANES - N_EXPERTS))).astype(F32)
            x2 = _moe(x2, _row(ffn_norm[l]), router_p, moe_w1[i].astype(BF16), moe_w3[i].astype(BF16),
                      moe_w2[i].astype(BF16), _row(final_norm), final_norm=(l == depth - 1))
    if depth % 2 == 1:
        raise NotImplementedError("final norm is fused into the last (MoE) layer")
    return x2.reshape(b, s, d)
```

```python
import functools

import jax
import jax.numpy as jnp
from jax import lax
from jax.experimental import pallas as pl
from jax.experimental.pallas import tpu as pltpu

F32 = jnp.float32
BF16 = jnp.bfloat16

D_MODEL = 1024
HEAD_DIM = 64
GROUP_WIDTH = 256
SWA_WINDOW = 128
ROPE_THETA = 10000.0
GLA_HEADS = 4
GLA_DK = 32
GLA_LOW_RANK = 16
GLA_TAU = 16.0
GLA_CHUNK = 64
SB_BLOCK = 128
RWKV_CHUNK = 64
RWKV_DECAY_SCALE = 0.6065306597126334
RWKV_GN_EPS = 64e-5
D_FF = 2816
N_EXPERTS = 8
D_FF_EXPERT = 1408
NORM_EPS = 1e-6

A_COLS = 512
B_COLS = 784
C_COLS = 768
D_COLS = 1024
B_COLS_PAD = 896

LANES = 128
VMEM_LIMIT_BYTES = 56 * 1024 * 1024


def _params(*sem):
    return pltpu.CompilerParams(dimension_semantics=sem, vmem_limit_bytes=VMEM_LIMIT_BYTES)


def _dot(a, b):
    return jnp.dot(a, b, preferred_element_type=F32)


def _dot_nt(a, b):
    return lax.dot_general(a, b, (((1,), (1,)), ((), ())), preferred_element_type=F32)


def _dot_tn(a, b):
    return lax.dot_general(a, b, (((0,), (0,)), ((), ())), preferred_element_type=F32)


def _dot_hi(a, b):
    return jnp.dot(a, b, preferred_element_type=F32, precision=lax.Precision.HIGHEST)


def _dot_nt_hi(a, b):
    return lax.dot_general(a, b, (((1,), (1,)), ((), ())), preferred_element_type=F32,
                           precision=lax.Precision.HIGHEST)


def _inv_dot(a, b):
    return _dot(a.astype(BF16), b.astype(BF16))


def _split_dot(a, b_exact):
    hi = a.astype(BF16)
    lo = (a - hi.astype(F32)).astype(BF16)
    return _dot(hi, b_exact) + _dot(lo, b_exact)


def _split_dot_tri(tri, a):
    hi = a.astype(BF16)
    lo = (a - hi.astype(F32)).astype(BF16)
    return _dot(tri, hi) + _dot(tri, lo)


def _iota(shape, dim):
    return lax.broadcasted_iota(jnp.int32, shape, dim)


def _idiv(x, d):
    return jnp.right_shift(x, d.bit_length() - 1)


def _imod(x, d):
    return jnp.bitwise_and(x, d - 1)


def _group_ones(width, group):
    r = _idiv(_iota((width, width), 0), group)
    c = _idiv(_iota((width, width), 1), group)
    return jnp.where(r == c, 1.0, 0.0).astype(BF16)


def _softplus(z):
    return jnp.maximum(z, 0.0) + jnp.log1p(jnp.exp(-jnp.abs(z)))


def _sigmoid(z):
    return 1.0 / (1.0 + jnp.exp(-z))


def _rms_scale(x):
    return x * lax.rsqrt(jnp.mean(x * x, axis=-1, keepdims=True) + NORM_EPS)


def _in_proj_kernel(x_ref, g_ref, wa_ref, wb_ref, wc_ref, wd_ref, pa_ref, pb_ref, pc_ref, pd_ref):
    h = (_rms_scale(x_ref[...]) * g_ref[...]).astype(BF16)
    pa_ref[...] = _dot(h, wa_ref[...])
    pb_ref[...] = _dot(h, wb_ref[...])
    pc_ref[...] = _dot(h, wc_ref[...]).astype(BF16)
    pd_ref[...] = _dot(h, wd_ref[...])


def _in_proj(x2, g, wa, wb, wc, wd, tm=512):
    n = x2.shape[0]
    full = lambda w: pl.BlockSpec(w.shape, lambda i: (0, 0))
    row = lambda c: pl.BlockSpec((tm, c), lambda i: (i, 0))
    return pl.pallas_call(
        _in_proj_kernel,
        grid=(n // tm,),
        in_specs=[row(D_MODEL), full(g), full(wa), full(wb), full(wc), full(wd)],
        out_specs=[row(A_COLS), row(B_COLS_PAD), row(C_COLS), row(D_COLS)],
        out_shape=[jax.ShapeDtypeStruct((n, A_COLS), F32), jax.ShapeDtypeStruct((n, B_COLS_PAD), F32),
                   jax.ShapeDtypeStruct((n, C_COLS), BF16), jax.ShapeDtypeStruct((n, D_COLS), F32)],
        compiler_params=_params("parallel"),
        name="in_proj",
    )(x2, g, wa, wb, wc, wd)


def _swa_kernel(sink_ref, cur_ref, prev_ref, cc_ref, sc_ref, cp_ref, sp_ref, gain_ref, o_ref):
    w = SWA_WINDOW
    blk = pl.program_id(1)
    lane = _iota((1, LANES), 1)
    first_half = _imod(lane, HEAD_DIM) < (HEAD_DIM // 2)
    low_head = lane < HEAD_DIM

    def rope(x, c, s):
        rot = jnp.where(first_half, pltpu.roll(x, LANES - HEAD_DIM // 2, 1), pltpu.roll(x, HEAD_DIM // 2, 1))
        return x * c + rot * s

    cc, sc = cc_ref[...], sc_ref[...]
    scale = HEAD_DIM ** -0.5
    q01 = rope(cur_ref[:, 0:128], cc, sc) * scale
    q23 = rope(cur_ref[:, 128:256], cc, sc) * scale
    k_cur = rope(cur_ref[:, 256:384], cc, sc)
    k_prev = rope(prev_ref[:, 256:384], cp_ref[...], sp_ref[...])
    kw = jnp.concatenate([k_prev, k_cur], axis=0).astype(BF16)
    vw = jnp.concatenate([prev_ref[:, 384:512], cur_ref[:, 384:512]], axis=0).astype(BF16)
    qs = jnp.concatenate([
        jnp.where(low_head, q01, 0.0),
        jnp.where(low_head, pltpu.roll(q01, HEAD_DIM, 1), 0.0),
        jnp.where(low_head, 0.0, pltpu.roll(q23, HEAD_DIM, 1)),
        jnp.where(low_head, 0.0, q23)], axis=0).astype(BF16)
    s_all = _dot_nt(qs, kw)

    qi = _iota((w, 2 * w), 0)
    kj = _iota((w, 2 * w), 1)
    valid = (kj > qi) & (kj <= qi + w) & ((blk > 0) | (kj >= w))
    heads = []
    for h in range(4):
        s = s_all[h * w:(h + 1) * w]
        sink = sink_ref[h]
        m = jnp.maximum(jnp.max(jnp.where(valid, s, -1e30), axis=-1, keepdims=True), sink)
        p = jnp.where(valid, jnp.exp(s - m), 0.0)
        den = jnp.sum(p, axis=-1, keepdims=True) + jnp.exp(sink - m)
        o = _dot(p.astype(BF16), vw) / den
        own = low_head if h < 2 else jnp.logical_not(low_head)
        ms = jnp.sum(jnp.where(own, o * o, 0.0), axis=-1, keepdims=True) * (1.0 / HEAD_DIM)
        heads.append(o * lax.rsqrt(ms + NORM_EPS))
    out01 = jnp.where(low_head, heads[0], pltpu.roll(heads[1], HEAD_DIM, 1))
    out23 = jnp.where(low_head, pltpu.roll(heads[2], HEAD_DIM, 1), heads[3])
    o_ref[...] = (jnp.concatenate([out01, out23], axis=1) * gain_ref[...]).astype(o_ref.dtype)


def _swa(pa, cos_t, sin_t, sinks, gain):
    b, s, _ = pa.shape
    w = SWA_WINDOW
    cur = lambda c: pl.BlockSpec((None, w, c), lambda bi, i: (bi, i, 0))
    prev = lambda c: pl.BlockSpec((None, w, c), lambda bi, i: (bi, jnp.maximum(i - 1, 0), 0))
    return pl.pallas_call(
        _swa_kernel,
        grid=(b, s // w),
        in_specs=[pl.BlockSpec(memory_space=pltpu.SMEM), cur(A_COLS), prev(A_COLS),
                  cur(LANES), cur(LANES), prev(LANES), prev(LANES),
                  pl.BlockSpec((1, GROUP_WIDTH), lambda bi, i: (0, 0))],
        out_specs=cur(GROUP_WIDTH),
        out_shape=jax.ShapeDtypeStruct((b, s, GROUP_WIDTH), BF16),
        compiler_params=_params("parallel", "parallel"),
        name="swa",
    )(sinks, pa, pa, cos_t, sin_t, cos_t, sin_t, gain)


def _gla_kernel(pb_ref, wa2_ref, ba_ref, gain_ref, o_ref, st_ref, ob_ref, la_ref):
    c_len = GLA_CHUNK
    t_len = pb_ref.shape[0]
    nh = GLA_HEADS

    @pl.when(pl.program_id(1) == 0)
    def _():
        st_ref[...] = jnp.zeros_like(st_ref)

    lr = pb_ref[:, 768:896].astype(BF16)
    la_ref[...] = -_softplus(-(_dot(lr, wa2_ref[...]) + ba_ref[...])) * (1.0 / GLA_TAU)

    tri = jnp.where(_iota((c_len, c_len), 0) >= _iota((c_len, c_len), 1), 1.0, 0.0).astype(BF16)
    klane = _idiv(_iota((1, LANES), 1), GLA_DK)
    vlane = _idiv(_iota((1, GROUP_WIDTH), 1), HEAD_DIM)
    r_i = _iota((nh * c_len, nh * c_len), 0)
    c_i = _iota((nh * c_len, nh * c_len), 1)
    causal = (_idiv(r_i, c_len) == _idiv(c_i, c_len)) & (_imod(r_i, c_len) >= _imod(c_i, c_len))
    st_mask = _idiv(_iota((GROUP_WIDTH, LANES), 0), HEAD_DIM) == _idiv(_iota((GROUP_WIDTH, LANES), 1), GLA_DK)

    def stack_k(x):
        return jnp.concatenate([jnp.where(klane == h, x, 0.0) for h in range(nh)], axis=0).astype(BF16)

    n = range(t_len // c_len)
    rows = [slice(ci * c_len, (ci + 1) * c_len) for ci in n]
    b = [_split_dot_tri(tri, la_ref[rows[i], :]) for i in n]
    b_mid = [b[i][c_len // 2 - 1:c_len // 2] for i in n]
    b_last = [b[i][c_len - 1:c_len] for i in n]
    q = [pb_ref[rows[i], 0:128] * (GLA_DK ** -0.5) for i in n]
    k = [pb_ref[rows[i], 128:256] for i in n]
    v = [pb_ref[rows[i], 256:512] for i in n]
    att = [_dot_nt(stack_k(q[i] * jnp.exp(b[i] - b_mid[i])), stack_k(k[i] * jnp.exp(b_mid[i] - b[i]))) for i in n]
    att = [jnp.where(causal, att[i], 0.0).astype(BF16) for i in n]
    v_st = [jnp.concatenate([jnp.where(vlane == h, v[i], 0.0) for h in range(nh)], axis=0).astype(BF16) for i in n]
    o_st = [_dot(att[i], v_st[i]) for i in n]
    kv = [_dot_tn(v[i].astype(BF16), (k[i] * jnp.exp(b_last[i] - b[i])).astype(BF16)) for i in n]
    st = st_ref[...]
    st_before = []
    for i in n:
        st_before.append(st.astype(BF16))
        st = st * jnp.exp(b_last[i]) + jnp.where(st_mask, kv[i], 0.0)
    st_ref[...] = st
    o_st = [o_st[i] + _dot_nt(stack_k(q[i] * jnp.exp(b[i])), st_before[i]) for i in n]
    for i in n:
        o = o_st[i][0:c_len]
        for h in range(1, nh):
            o = o + o_st[i][h * c_len:(h + 1) * c_len]
        ob_ref[rows[i], :] = o

    ob = ob_ref[...]
    ms = _split_dot(ob * ob, _group_ones(GROUP_WIDTH, HEAD_DIM)) * (1.0 / HEAD_DIM)
    g = pb_ref[:, 512:768]
    y = ob * lax.rsqrt(ms + NORM_EPS) * gain_ref[...] * (g * _sigmoid(g))
    o_ref[...] = y.astype(o_ref.dtype)


def _gla(pb, wa2, ba, gain, tile=512):
    b, s, _ = pb.shape
    full = lambda a: pl.BlockSpec(a.shape, lambda bi, i: (0, 0))
    return pl.pallas_call(
        _gla_kernel,
        grid=(b, s // tile),
        in_specs=[pl.BlockSpec((None, tile, B_COLS_PAD), lambda bi, i: (bi, i, 0)), full(wa2), full(ba), full(gain)],
        out_specs=pl.BlockSpec((None, tile, GROUP_WIDTH), lambda bi, i: (bi, i, 0)),
        out_shape=jax.ShapeDtypeStruct((b, s, GROUP_WIDTH), BF16),
        scratch_shapes=[pltpu.VMEM((GROUP_WIDTH, LANES), F32), pltpu.VMEM((tile, GROUP_WIDTH), F32),
                        pltpu.VMEM((tile, LANES), F32)],
        compiler_params=_params("parallel", "arbitrary"),
        name="gla",
    )(pb, wa2, ba, gain)


def _sb_kernel(pc_ref, gain_ref, o_ref, kcat_ref, vcat_ref, *, tq):
    t = SB_BLOCK
    nh = 4
    nq = tq // t
    qb = pl.program_id(1)
    nblk = pc_ref.shape[0] // t
    hlane = _idiv(_iota((1, GROUP_WIDTH), 1), HEAD_DIM)

    @pl.when(qb == 0)
    def _():
        def fill(kb, carry):
            rows = pl.ds(pl.multiple_of(kb * t, t), t)
            k = pc_ref[rows, 256:512]
            v = pc_ref[rows, 512:768]
            for h in range(nh):
                kcat_ref[kb, h * t:(h + 1) * t, :] = jnp.where(hlane == h, k, jnp.zeros_like(k))
                vcat_ref[kb, h * t:(h + 1) * t, :] = jnp.where(hlane == h, v, jnp.zeros_like(v))
            return carry
        lax.fori_loop(0, nblk, fill, 0)

    q = pc_ref[pl.ds(pl.multiple_of(qb * tq, tq), tq), 0:256] * (HEAD_DIM ** -0.5)
    uj = jnp.where((_iota((t, 2 * t), 0) > _iota((t, 2 * t), 1)) | (_iota((t, 2 * t), 1) >= t), 1.0, 0.0).astype(BF16)
    key_in_blk = _imod(_iota((tq, nh * t), 1), t)
    row_in_tile = _iota((tq, nh * t), 0)

    def scores(kb):
        return _dot_nt(q, kcat_ref[kb])

    def block(kb, z, c, acc, strict):
        sp = jnp.maximum(z, 0.0) + jnp.log(1.0 + jnp.exp(-jnp.abs(z)))
        if strict is not None:
            sp = jnp.where(strict, sp, 0.0)
        args, c_new = [], []
        for h in range(nh):
            sl = slice(h * t, (h + 1) * t)
            cj = _dot(sp[:, sl].astype(BF16), uj)
            args.append(z[:, sl] - sp[:, sl] - cj[:, :t] - c[h])
            c_new.append(c[h] + cj[:, t:])
        attn = jnp.exp(jnp.concatenate(args, axis=1))
        if strict is not None:
            attn = jnp.where(strict, attn, 0.0)
        return tuple(c_new), attn.astype(BF16)

    c = (jnp.zeros((tq, t), F32),) * nh
    acc = jnp.zeros((tq, GROUP_WIDTH), F32)
    attn = None
    for d in range(nq - 1, -1, -1):
        kb = qb * nq + d
        z = scores(kb)
        if attn is not None:
            acc = acc + _dot(attn, vcat_ref[kb + 1])
        c, attn = block(kb, z, c, acc, key_in_blk + d * t < row_in_tile)

    def body(i, carry):
        c, acc, attn = carry
        kb = qb * nq - 1 - i
        z = scores(kb)
        acc = acc + _dot(attn, vcat_ref[kb + 1])
        c, attn = block(kb, z, c, acc, None)
        return c, acc, attn

    c, acc, attn = lax.fori_loop(0, qb * nq, body, (c, acc, attn))
    acc = acc + _dot(attn, vcat_ref[0])
    ms = _split_dot(acc * acc, _group_ones(GROUP_WIDTH, HEAD_DIM)) * (1.0 / HEAD_DIM)
    o_ref[...] = (acc * lax.rsqrt(ms + NORM_EPS) * gain_ref[...]).astype(o_ref.dtype)


def _sb(pc, gain, tq=512):
    b, s, _ = pc.shape
    t = SB_BLOCK
    return pl.pallas_call(
        functools.partial(_sb_kernel, tq=tq),
        grid=(b, s // tq),
        in_specs=[pl.BlockSpec((None, s, C_COLS), lambda bi, i: (bi, 0, 0)),
                  pl.BlockSpec((1, GROUP_WIDTH), lambda bi, i: (0, 0))],
        out_specs=pl.BlockSpec((None, tq, GROUP_WIDTH), lambda bi, i: (bi, i, 0)),
        out_shape=jax.ShapeDtypeStruct((b, s, GROUP_WIDTH), BF16),
        scratch_shapes=[pltpu.VMEM((s // t, 4 * t, GROUP_WIDTH), BF16), pltpu.VMEM((s // t, 4 * t, GROUP_WIDTH), BF16)],
        compiler_params=_params("parallel", "arbitrary"),
        name="stick_breaking",
    )(pc, gain)


def _rwkv_kernel(pd_ref, mu_ref, w0_ref, w2_ref, a0_ref, a2_ref, g2_ref, kk_ref, ka_ref, rk_ref, lnw_ref, lnb_ref,
                 o_ref, st_ref, carry_ref, r_s, k_s, v_s, kk_s, b_s, lw_s, y_s,
                 l2_s, w_s, y0_s, brb_s, vk_s, bl_s, el_s):
    c_len = RWKV_CHUNK
    t_len = pd_ref.shape[0]
    gw = GROUP_WIDTH

    @pl.when(pl.program_id(1) == 0)
    def _():
        st_ref[...] = jnp.zeros_like(st_ref)
        carry_ref[...] = jnp.zeros_like(carry_ref)

    cur = pd_ref[...]
    shifted = jnp.where(_iota((t_len, 1), 0) == 0, carry_ref[7:8, :], pltpu.roll(cur, 1, 0))
    carry_ref[...] = cur[t_len - 8:t_len, :]
    xm = cur + (shifted - cur) * mu_ref[...]
    r = xm[:, 0:gw]
    k = xm[:, gw:2 * gw]
    v = xm[:, 2 * gw:3 * gw]
    wa = xm[:, 3 * gw:3 * gw + 128]
    gl = xm[:, 3 * gw + 128:4 * gw]
    lora_lo = _iota((1, LANES), 1) < 64
    wl = jnp.where(lora_lo, jnp.tanh(wa), 0.0).astype(BF16)
    lw_s[...] = -RWKV_DECAY_SCALE * _sigmoid(w0_ref[...] + _dot(wl, w2_ref[...]))
    a = _sigmoid(a0_ref[...] + _dot(wa.astype(BF16), a2_ref[...]))
    g = _dot(_sigmoid(gl).astype(BF16), g2_ref[...])
    ones = _group_ones(gw, HEAD_DIM)
    kk = k * kk_ref[...]
    kk = kk * lax.rsqrt(_split_dot(kk * kk, ones) + 1e-12)
    k2 = k * (1.0 + (a - 1.0) * ka_ref[...])
    r_s[...] = r
    k_s[...] = k2
    v_s[...] = v
    kk_s[...] = kk
    b_s[...] = kk * a
    bonus = _split_dot(r * k2 * rk_ref[...], ones) * v

    tri = jnp.where(_iota((c_len, c_len), 0) >= _iota((c_len, c_len), 1), 1.0, 0.0).astype(BF16)
    head0 = _iota((1, LANES), 1) < HEAD_DIM
    r_i = _iota((2 * c_len, 2 * c_len), 0)
    c_i = _iota((2 * c_len, 2 * c_len), 1)
    same = _idiv(r_i, c_len) == _idiv(c_i, c_len)
    strict = same & (_imod(r_i, c_len) > _imod(c_i, c_len))
    incl = same & (_imod(r_i, c_len) >= _imod(c_i, c_len))
    eye = jnp.where(r_i == c_i, 1.0, 0.0)

    def stack(x):
        return jnp.concatenate([jnp.where(head0, x, 0.0), jnp.where(head0, 0.0, x)], axis=0)

    b16 = lambda x: x.astype(BF16)
    two_c = 2 * c_len

    def load_chunk(ci, p):
        rows = pl.ds(pl.multiple_of(ci * c_len, c_len), c_len)
        ln = slice(p * LANES, (p + 1) * LANES)
        return tuple(s[rows, ln] for s in (lw_s, r_s, kk_s, k_s, b_s, v_s))

    def prepare(chains):
        n = range(len(chains))
        lw, r_c, kk_c, k_c, b_c, v_c = zip(*chains)
        cum = [_split_dot_tri(tri, lw[i]) for i in n]
        ref = [cum[i][c_len // 2 - 1:c_len // 2] for i in n]
        last = [cum[i][c_len - 1:c_len] for i in n]
        e_neg = [jnp.exp(ref[i] - cum[i]) for i in n]
        rd = [r_c[i] * jnp.exp(cum[i] - ref[i]) for i in n]
        kkd = [kk_c[i] * jnp.exp(cum[i] - lw[i] - ref[i]) for i in n]
        kd = [k_c[i] * e_neg[i] for i in n]
        bd = [b_c[i] * e_neg[i] for i in n]
        e_ref = [jnp.exp(ref[i]) for i in n]
        e_lr = [jnp.exp(last[i] - ref[i]) for i in n]
        lhs = [b16(jnp.concatenate([stack(kkd[i]), stack(rd[i])], axis=0)) for i in n]
        gk = [_dot_nt(lhs[i], b16(stack(kd[i]))) for i in n]
        gb = [_dot_nt(lhs[i], b16(stack(bd[i]))) for i in n]
        a_kk = [b16(jnp.where(strict, gk[i][0:two_c], 0.0)) for i in n]
        a_bk = [jnp.where(strict, gb[i][0:two_c], 0.0) for i in n]
        b_rk = [b16(jnp.where(incl, gk[i][two_c:], 0.0)) for i in n]
        b_rb = [b16(jnp.where(incl, gb[i][two_c:], 0.0)) for i in n]
        v_st = [b16(stack(v_c[i])) for i in n]
        akv = [b16(_dot(a_kk[i], v_st[i])) for i in n]
        y0 = [_dot(b_rk[i], v_st[i]) for i in n]
        vk = [_dot_tn(v_st[i], b16(stack(kd[i] * e_lr[i]))) for i in n]
        inv = [eye - a_bk[i] for i in n]
        pw = [_inv_dot(a_bk[i], a_bk[i]) for i in n]
        n_fac = c_len.bit_length() - 2
        for it in range(n_fac):
            inv = [inv[i] + _inv_dot(pw[i], inv[i]) for i in n]
            if it < n_fac - 1:
                pw = [_inv_dot(pw[i], pw[i]) for i in n]
        inv_b = [b16(inv[i]) for i in n]
        w = [_dot(inv_b[i], akv[i]) for i in n]
        u = [b16(_dot(inv_b[i], b16(stack(kkd[i] * e_ref[i])))) for i in n]
        return [(jnp.concatenate([u[i], b16(stack(rd[i] * e_ref[i]))], axis=0),
                 w[i], y0[i], b_rb[i], vk[i], b16(stack(bd[i] * e_lr[i])),
                 jnp.broadcast_to(jnp.exp(last[i]), (8, LANES))) for i in n]

    prep_refs = (l2_s, w_s, y0_s, brb_s, vk_s, bl_s, el_s)
    group = 4

    def prepare_group(i, carry):
        idx = [(group * i + dc, p) for dc in range(group) for p in range(2)]
        outs = prepare([load_chunk(ci, p) for ci, p in idx])
        for (ci, p), out in zip(idx, outs):
            for ref, val in zip(prep_refs, out):
                ref[ci * 2 + p] = val
        return carry

    lax.fori_loop(0, t_len // (group * c_len), prepare_group, 0)

    def advance(ci, carry):
        rows = pl.ds(pl.multiple_of(ci * c_len, c_len), c_len)
        pr = range(2)
        n0 = [st_ref[p] for p in pr]
        l2, w, y0, brb, vk, bl, el = zip(*[tuple(ref[ci * 2 + p] for ref in prep_refs) for p in pr])
        kr = [_dot_nt(l2[p], b16(n0[p])) for p in pr]
        sa_b = [b16(kr[p][0:two_c] + w[p]) for p in pr]
        y_st = [kr[p][two_c:] + y0[p] - _dot(brb[p], sa_b[p]) for p in pr]
        n1 = [n0[p] * el[p][0:1, :] + vk[p] - _dot_tn(sa_b[p], bl[p]) for p in pr]
        for p in pr:
            y_s[rows, p * LANES:(p + 1) * LANES] = y_st[p][0:c_len] + y_st[p][c_len:]
            st_ref[p] = n1[p]
        return carry

    lax.fori_loop(0, t_len // c_len, advance, 0)

    y = y_s[...]
    mean = _split_dot(y, ones) * (1.0 / HEAD_DIM)
    yc = y - mean
    var = _split_dot(yc * yc, ones) * (1.0 / HEAD_DIM)
    y = yc * lax.rsqrt(var + RWKV_GN_EPS) * lnw_ref[...] + lnb_ref[...]
    o_ref[...] = ((y + bonus) * g).astype(o_ref.dtype)


def _rwkv(pd, mu, w0, w2p, a0, a2p, g2, k_k, k_a, r_k, ln_w, ln_b, tile=512):
    b, s, _ = pd.shape
    full = lambda a: pl.BlockSpec(a.shape, lambda bi, i: (0, 0))
    params = (mu, w0, w2p, a0, a2p, g2, k_k, k_a, r_k, ln_w, ln_b)
    sc = lambda: pltpu.VMEM((tile, GROUP_WIDTH), F32)
    c = RWKV_CHUNK
    nj = 2 * (tile // c)
    return pl.pallas_call(
        _rwkv_kernel,
        grid=(b, s // tile),
        in_specs=[pl.BlockSpec((None, tile, D_COLS), lambda bi, i: (bi, i, 0))] + [full(a) for a in params],
        out_specs=pl.BlockSpec((None, tile, GROUP_WIDTH), lambda bi, i: (bi, i, 0)),
        out_shape=jax.ShapeDtypeStruct((b, s, GROUP_WIDTH), BF16),
        scratch_shapes=[pltpu.VMEM((2, LANES, LANES), F32), pltpu.VMEM((8, D_COLS), F32),
                        sc(), sc(), sc(), sc(), sc(), sc(), sc(),
                        pltpu.VMEM((nj, 4 * c, LANES), BF16),
                        pltpu.VMEM((nj, 2 * c, LANES), F32), pltpu.VMEM((nj, 2 * c, LANES), F32),
                        pltpu.VMEM((nj, 2 * c, 2 * c), BF16), pltpu.VMEM((nj, LANES, LANES), F32),
                        pltpu.VMEM((nj, 2 * c, LANES), BF16), pltpu.VMEM((nj, 8, LANES), F32)],
        compiler_params=_params("parallel", "arbitrary"),
        name="rwkv7",
    )(pd, *params)


def _out_proj_kernel(x_ref, oa_ref, ob_ref, oc_ref, od_ref, w_ref, o_ref):
    gw = GROUP_WIDTH
    acc = x_ref[...]
    for i, m_ref in enumerate((oa_ref, ob_ref, oc_ref, od_ref)):
        acc = acc + _dot(m_ref[...], w_ref[i * gw:(i + 1) * gw, :])
    o_ref[...] = acc


def _out_proj(x2, oa, ob, oc, od, w_out, tm=512):
    n = x2.shape[0]
    row = lambda c: pl.BlockSpec((tm, c), lambda i: (i, 0))
    return pl.pallas_call(
        _out_proj_kernel,
        grid=(n // tm,),
        in_specs=[row(D_MODEL)] + [row(GROUP_WIDTH)] * 4 + [pl.BlockSpec(w_out.shape, lambda i: (0, 0))],
        out_specs=row(D_MODEL),
        out_shape=jax.ShapeDtypeStruct((n, D_MODEL), F32),
        compiler_params=_params("parallel"),
        name="out_proj",
    )(x2, oa, ob, oc, od, w_out)


def _swiglu_chunks(h, w1_ref, w3_ref, w2_ref, d_ff, chunk=256):
    out = None
    for lo in range(0, d_ff, chunk):
        hi = min(lo + chunk, d_ff)
        a = _dot(h, w1_ref[:, lo:hi])
        b = _dot(h, w3_ref[:, lo:hi])
        act = (a * _sigmoid(a) * b).astype(BF16)
        y = _dot(act, w2_ref[lo:hi, :])
        out = y if out is None else out + y
    return out


def _ffn_kernel(x_ref, g_ref, w1_ref, w3_ref, w2_ref, o_ref):
    x = x_ref[...]
    h = (_rms_scale(x) * g_ref[...]).astype(BF16)
    o_ref[...] = x + _swiglu_chunks(h, w1_ref, w3_ref, w2_ref, D_FF)


def _ffn(x2, g, w1, w3, w2, tm=512):
    n = x2.shape[0]
    row = pl.BlockSpec((tm, D_MODEL), lambda i: (i, 0))
    full = lambda a: pl.BlockSpec(a.shape, lambda i: (0, 0))
    return pl.pallas_call(
        _ffn_kernel,
        grid=(n // tm,),
        in_specs=[row, full(g), full(w1), full(w3), full(w2)],
        out_specs=row,
        out_shape=jax.ShapeDtypeStruct((n, D_MODEL), F32),
        compiler_params=_params("parallel"),
        name="ffn_dense",
    )(x2, g, w1, w3, w2)


def _moe_kernel(x_ref, g_ref, router_ref, w1_ref, w3_ref, w2_ref, fin_ref, o_ref,
                h_ref, comb_ref, combt_ref, rank_ref, rankt_ref, acc_ref, *, final_norm, cap):
    e = pl.program_id(1)
    tm = x_ref.shape[0]
    lane = _iota((1, LANES), 1).astype(F32)

    @pl.when(e == 0)
    def _():
        hf = _rms_scale(x_ref[...]) * g_ref[...]
        h_hi = hf.astype(BF16)
        h_ref[...] = h_hi
        h_lo = (hf - h_hi.astype(F32)).astype(BF16)
        r_hi = router_ref[0]
        logits = _dot(h_hi, r_hi) + _dot(h_lo, r_hi) + _dot(h_hi, router_ref[1])
        logits = jnp.where(lane < N_EXPERTS, logits, -jnp.inf)
        m1 = jnp.max(logits, axis=-1, keepdims=True)
        i1 = jnp.min(jnp.where(logits == m1, lane, float(LANES)), axis=-1, keepdims=True)
        rest = jnp.where(lane == i1, -jnp.inf, logits)
        m2 = jnp.max(rest, axis=-1, keepdims=True)
        i2 = jnp.min(jnp.where(rest == m2, lane, float(LANES)), axis=-1, keepdims=True)
        e2 = jnp.exp(m2 - m1)
        g1 = 1.0 / (1.0 + e2)
        comb = jnp.where(lane == i1, g1, 0.0) + jnp.where(lane == i2, e2 * g1, 0.0)
        comb_ref[...] = comb
        combt_ref[...] = comb.T
        earlier = jnp.where(_iota((tm, tm), 0) > _iota((tm, tm), 1), 1.0, 0.0).astype(BF16)
        rank = _dot(earlier, jnp.where(comb > 0.0, 1.0, 0.0).astype(BF16))
        rank_ref[...] = rank
        rankt_ref[...] = rank.T
        acc_ref[...] = jnp.zeros_like(acc_ref)

    mine = lane == e.astype(F32)
    gate_col = jnp.sum(jnp.where(mine, comb_ref[...], 0.0), axis=-1, keepdims=True)
    rank_col = jnp.sum(jnp.where(mine, rank_ref[...], 0.0), axis=-1, keepdims=True)
    gate_row = combt_ref[pl.ds(e, 1), :]
    rank_row = rankt_ref[pl.ds(e, 1), :]
    routed = jnp.sum(jnp.where(gate_row > 0.0, 1.0, 0.0)).astype(jnp.int32)
    n_pass = (routed + (cap - 1)) // cap

    def one_pass(p, carry):
        base = (p * cap).astype(F32)
        slot_sub = _iota((cap, 1), 0).astype(F32) + base
        pick = jnp.where((rank_row == slot_sub) & (gate_row > 0.0), 1.0, 0.0).astype(BF16)
        xg = _dot(pick, h_ref[...]).astype(BF16)
        y = _swiglu_chunks(xg, w1_ref, w3_ref, w2_ref, D_FF_EXPERT).astype(BF16)
        slot_lane = _iota((1, cap), 1).astype(F32) + base
        put = jnp.where((rank_col == slot_lane) & (gate_col > 0.0), 1.0, 0.0).astype(BF16)
        acc_ref[...] += gate_col * _dot(put, y)
        return carry

    lax.fori_loop(0, n_pass, one_pass, 0)

    @pl.when(e == N_EXPERTS - 1)
    def _():
        y = x_ref[...] + acc_ref[...]
        if final_norm:
            y = _rms_scale(y) * fin_ref[...]
        o_ref[...] = y


def _moe(x2, g, router_p, w1, w3, w2, fin, final_norm, tm=1024, cap=384):
    n = x2.shape[0]
    row = pl.BlockSpec((tm, D_MODEL), lambda i, e: (i, 0))
    full = lambda a: pl.BlockSpec(a.shape, lambda i, e: (0,) * a.ndim)
    exp = lambda a: pl.BlockSpec((None,) + a.shape[1:], lambda i, e: (e, 0, 0))
    router_hi = router_p.astype(BF16)
    router_p = jnp.stack([router_hi, (router_p - router_hi.astype(F32)).astype(BF16)])
    return pl.pallas_call(
        functools.partial(_moe_kernel, final_norm=final_norm, cap=cap),
        grid=(n // tm, N_EXPERTS),
        in_specs=[row, full(g), full(router_p), exp(w1), exp(w3), exp(w2), full(fin)],
        out_specs=row,
        out_shape=jax.ShapeDtypeStruct((n, D_MODEL), F32),
        scratch_shapes=[pltpu.VMEM((tm, D_MODEL), BF16), pltpu.VMEM((tm, LANES), F32), pltpu.VMEM((LANES, tm), F32),
                        pltpu.VMEM((tm, LANES), F32), pltpu.VMEM((LANES, tm), F32), pltpu.VMEM((tm, D_MODEL), F32)],
        compiler_params=_params("parallel", "arbitrary"),
        name="moe",
    )(x2, g, router_p, w1, w3, w2, fin)


def _rope_tables(positions):
    inv_freq = ROPE_THETA ** (-jnp.arange(0, HEAD_DIM, 2, dtype=F32) / HEAD_DIM)
    ang = positions.astype(F32)[..., None] * inv_freq
    cos, sin = jnp.cos(ang), jnp.sin(ang)
    cos_t = jnp.concatenate([cos, cos, cos, cos], axis=-1)
    sin_t = jnp.concatenate([-sin, sin, -sin, sin], axis=-1)
    return cos_t, sin_t


def _row(v):
    return v.reshape(1, -1).astype(F32)


def kernel(x, positions, attn_norm, w_in, swa_sinks, swa_out_norm, gla_w_a2, gla_b_a, gla_out_norm, sb_out_norm, rwkv_mu, rwkv_w0, rwkv_w2, rwkv_a0, rwkv_a2, rwkv_g2, rwkv_k_k, rwkv_k_a, rwkv_r_k, rwkv_ln_w, rwkv_ln_b, w_out, ffn_norm, ffn_w1, ffn_w3, ffn_w2, router, moe_w1, moe_w3, moe_w2, final_norm):
    b, s, d = x.shape
    n = b * s
    depth = w_in.shape[0]
    cos_t, sin_t = _rope_tables(positions)
    x2 = x.reshape(n, d)
    for l in range(depth):
        w = w_in[l].astype(BF16)
        wa = w[:, :A_COLS]
        wb = jnp.pad(w[:, A_COLS:A_COLS + B_COLS], ((0, 0), (0, B_COLS_PAD - B_COLS)))
        wc = w[:, A_COLS + B_COLS:A_COLS + B_COLS + C_COLS]
        wd = w[:, A_COLS + B_COLS + C_COLS:]
        pa, pb, pc, pd = _in_proj(x2, _row(attn_norm[l]), wa, wb, wc, wd)

        oa = _swa(pa.reshape(b, s, A_COLS), cos_t, sin_t, swa_sinks[l].astype(F32), _row(swa_out_norm[l]))
        wa2 = jnp.pad(gla_w_a2[l], ((0, LANES - GLA_LOW_RANK), (0, 0))).astype(BF16)
        ob = _gla(pb.reshape(b, s, B_COLS_PAD), wa2, _row(gla_b_a[l]), _row(gla_out_norm[l]))
        oc = _sb(pc.reshape(b, s, C_COLS), _row(sb_out_norm[l]))
        lora = rwkv_w2.shape[1]
        w2p = jnp.pad(rwkv_w2[l], ((0, LANES - lora), (0, 0))).astype(BF16)
        a2p = jnp.pad(rwkv_a2[l], ((LANES - lora, 0), (0, 0))).astype(BF16)
        od = _rwkv(pd.reshape(b, s, D_COLS), _row(rwkv_mu[l]), _row(rwkv_w0[l]), w2p, _row(rwkv_a0[l]), a2p,
                   rwkv_g2[l].astype(BF16), _row(rwkv_k_k[l]), _row(rwkv_k_a[l]), _row(rwkv_r_k[l]),
                   _row(rwkv_ln_w[l]), _row(rwkv_ln_b[l]))

        gw = GROUP_WIDTH
        x2 = _out_proj(x2, oa.reshape(n, gw), ob.reshape(n, gw), oc.reshape(n, gw), od.reshape(n, gw),
                       w_out[l].astype(BF16))
        if l % 2 == 0:
            i = l // 2
            x2 = _ffn(x2, _row(ffn_norm[l]), ffn_w1[i].astype(BF16), ffn_w3[i].astype(BF16), ffn_w2[i].astype(BF16))
        else:
            i = l // 2
            router_p = jnp.pad(router[i], ((0, 0), (0, LANES - N_EXPERTS))).astype(F32)
            x2 = _moe(x2, _row(ffn_norm[l]), router_p, moe_w1[i].astype(BF16), moe_w3[i].astype(BF16),
                      moe_w2[i].astype(BF16), _row(final_norm), final_norm=(l == depth - 1))
    if depth % 2 == 1:
        raise NotImplementedError("final norm is fused into the last (MoE) layer")
    return x2.reshape(b, s, d)
```
